```python
import numpy as np
import jax
import jax.numpy as jnp
from jax import lax

D_MODEL = 1024
BATCH = 4
SEQ = 4096
DEPTH = 1

NSA_HEADS = 8
NSA_KV_GROUPS = 2
NSA_GROUP = NSA_HEADS // NSA_KV_GROUPS
NSA_HEAD_DIM = 64
ROPE_DIM = NSA_HEAD_DIM // 4
ROPE_THETA = 500000.0
CMP_BLOCK = 32
CMP_STRIDE = 16
CMP_HIDDEN = 256
SLC_BLOCK = 64
SLC_TOPK = 16
WINDOW = 512
Q_BLOCK = 128

RET_HEADS = 4
RET_QK_DIM = 128
RET_V_DIM = 256
RET_CHUNK = 128
RET_ROPE_THETA = 10000.0

D_FF = 2816
CONV_WIDTH = 3

EPS = 1e-6
NEG = -1e30

NSA_Q_W = NSA_HEADS * NSA_HEAD_DIM
NSA_KV_W = NSA_KV_GROUPS * NSA_HEAD_DIM
RET_QK_W = RET_HEADS * RET_QK_DIM
RET_V_W = RET_HEADS * RET_V_DIM
IN_SPLITS = (NSA_Q_W, 6 * NSA_KV_W, 3 * NSA_HEADS, RET_QK_W, RET_QK_W, RET_V_W, RET_V_W, 2 * D_MODEL)
IN_PROJ_W = sum(IN_SPLITS)

kernel_name = 'hybrid_nsa_retention_convffn'


def rms_norm(x, w):
    xf = x.astype(jnp.float32)
    y = xf * lax.rsqrt(jnp.mean(xf * xf, axis=-1, keepdims=True) + EPS)
    return (y * w.astype(jnp.float32)).astype(x.dtype)


def rope(x, rot_dim, theta):
    S = x.shape[1]
    half = rot_dim // 2
    inv = jnp.power(jnp.float32(theta), -jnp.arange(half, dtype=jnp.float32) / half)
    ang = jnp.arange(S, dtype=jnp.float32)[:, None] * inv[None, :]
    cos = jnp.cos(ang)[None, :, None, :]
    sin = jnp.sin(ang)[None, :, None, :]
    xr = x[..., :rot_dim].astype(jnp.float32)
    x1, x2 = xr[..., :half], xr[..., half:]
    rot = jnp.concatenate([x1 * cos - x2 * sin, x1 * sin + x2 * cos], axis=-1).astype(x.dtype)
    return jnp.concatenate([rot, x[..., rot_dim:]], axis=-1)


def masked_softmax(s, mask):
    p = jax.nn.softmax(jnp.where(mask, s, NEG), axis=-1)
    return jnp.where(mask, p, 0.0)


def nsa_attention(q, k_cmp, v_cmp, k_slc, v_slc, k_win, v_win, gates,
                  pe_k, w1_k, w2_k, pe_v, w1_v, w2_v):
    B, S = q.shape[0], q.shape[1]
    H, G, dh = NSA_KV_GROUPS, NSA_GROUP, NSA_HEAD_DIM
    scale = dh ** -0.5
    pos = jnp.arange(S)
    qg = q.reshape(B, S, H, G, dh)

    n_cmp = (S - CMP_BLOCK) // CMP_STRIDE + 1
    cmp_start = np.arange(n_cmp) * CMP_STRIDE
    cmp_idx = cmp_start[:, None] + np.arange(CMP_BLOCK)[None, :]

    def compress(t, pe, w1, w2):
        blk = t[:, cmp_idx] + pe[:, None, :]
        blk = jnp.swapaxes(blk, 2, 3).reshape(B, n_cmp, H, CMP_BLOCK * dh)
        return jax.nn.gelu(blk @ w1) @ w2

    kc = compress(k_cmp, pe_k, w1_k, w2_k)
    vc = compress(v_cmp, pe_v, w1_v, w2_v)
    s_cmp = jnp.einsum('bshgd,bchd->bhgsc', qg, kc).astype(jnp.float32) * scale
    mask_cmp = jnp.asarray(cmp_start + CMP_BLOCK - 1)[None, :] <= pos[:, None]
    p_cmp = masked_softmax(s_cmp, mask_cmp)
    o_cmp = jnp.einsum('bhgsc,bchd->bshgd', p_cmp.astype(vc.dtype), vc)

    n_slc = S // SLC_BLOCK
    slc_start = np.arange(n_slc) * SLC_BLOCK
    overlap = np.clip(np.minimum(cmp_start[:, None] + CMP_BLOCK, slc_start[None, :] + SLC_BLOCK)
                      - np.maximum(cmp_start[:, None], slc_start[None, :]), 0, None) / CMP_BLOCK
    imp = jnp.einsum('bhgsc,cj->bhsj', p_cmp, jnp.asarray(overlap, dtype=jnp.float32))
    cur = (pos // SLC_BLOCK)[:, None]
    jb = jnp.arange(n_slc)[None, :]
    forced = (jb == 0) | (jb == cur) | (jb == cur - 1)
    valid = jb <= cur
    imp = jnp.where(forced, jnp.inf, jnp.where(valid, imp, -jnp.inf))
    top_n = min(SLC_TOPK, n_slc)
    _, sel = lax.top_k(imp, top_n)

    k_blocks = jnp.swapaxes(k_slc.reshape(B, n_slc, SLC_BLOCK, H, dh), 1, 3)
    k_blocks = jnp.swapaxes(k_blocks, 2, 3)
    v_blocks = jnp.swapaxes(jnp.swapaxes(v_slc.reshape(B, n_slc, SLC_BLOCK, H, dh), 1, 3), 2, 3)
    n_q = S // Q_BLOCK
    q_chunks = jnp.moveaxis(qg.reshape(B, n_q, Q_BLOCK, H, G, dh), 1, 0)
    sel_chunks = jnp.moveaxis(sel.reshape(B, H, n_q, Q_BLOCK, top_n), 2, 0)
    pos_chunks = pos.reshape(n_q, Q_BLOCK)
    gather = jax.vmap(jax.vmap(lambda blocks, ix: blocks[ix]))

    def selected_block(args):
        qb, ib, tb = args
        kg = gather(k_blocks, ib)
        vg = gather(v_blocks, ib)
        s = jnp.einsum('bqhgd,bhqnkd->bhgqnk', qb, kg).astype(jnp.float32) * scale
        kpos = ib[..., None] * SLC_BLOCK + jnp.arange(SLC_BLOCK)
        mask = (kpos <= tb[None, None, :, None, None]).reshape(B, H, 1, Q_BLOCK, top_n * SLC_BLOCK)
        p = masked_softmax(s.reshape(B, H, G, Q_BLOCK, top_n * SLC_BLOCK), mask)
        return jnp.einsum('bhgqm,bhqmd->bqhgd', p.astype(vg.dtype),
                          vg.reshape(B, H, Q_BLOCK, top_n * SLC_BLOCK, dh))

    o_slc = lax.map(selected_block, (q_chunks, sel_chunks, pos_chunks))
    o_slc = jnp.moveaxis(o_slc, 0, 1).reshape(B, S, H, G, dh)

    kw = jnp.pad(k_win, ((0, 0), (WINDOW, 0), (0, 0), (0, 0)))
    vw = jnp.pad(v_win, ((0, 0), (WINDOW, 0), (0, 0), (0, 0)))
    band_len = Q_BLOCK + WINDOW
    band = np.arange(n_q)[:, None] * Q_BLOCK + np.arange(band_len)[None, :]
    kb = kw[:, band]
    vb = vw[:, band]
    qb = qg.reshape(B, n_q, Q_BLOCK, H, G, dh)
    s_win = jnp.einsum('bnqhgd,bnkhd->bnhgqk', qb, kb).astype(jnp.float32) * scale
    qi = np.arange(Q_BLOCK)[:, None]
    ki = np.arange(band_len)[None, :]
    k_abs = np.arange(n_q)[:, None, None] * Q_BLOCK + ki[None] - WINDOW
    win_mask = ((ki > qi) & (ki <= qi + WINDOW))[None] & (k_abs >= 0)
    p_win = masked_softmax(s_win, jnp.asarray(win_mask)[None, :, None, None])
    o_win = jnp.einsum('bnhgqk,bnkhd->bnqhgd', p_win.astype(vb.dtype), vb).reshape(B, S, H, G, dh)

    g = gates.reshape(B, S, H, G, 3)
    o = g[..., 0:1] * o_cmp + g[..., 1:2] * o_slc + g[..., 2:3] * o_win
    return o.reshape(B, S, NSA_HEADS * dh)


def retention(q, k, v):
    B, S, H, dk = q.shape
    dv = v.shape[-1]
    C = RET_CHUNK
    n = S // C
    f32 = jnp.float32
    q = rope(q, dk, RET_ROPE_THETA).astype(f32)
    k = rope(k, dk, RET_ROPE_THETA).astype(f32) * (dk ** -0.5)
    v = v.astype(f32)
    log_gamma = jnp.log1p(-jnp.exp2(-5.0 - jnp.arange(H, dtype=f32)))
    idx = jnp.arange(C, dtype=f32)
    diff = idx[:, None] - idx[None, :]
    decay_in = jnp.where(diff >= 0, jnp.exp(log_gamma[:, None, None] * jnp.maximum(diff, 0.0)), 0.0)
    q_decay = jnp.exp(log_gamma[None, :] * (idx[:, None] + 1.0))
    k_decay = jnp.exp(log_gamma[None, :] * (C - 1.0 - idx[:, None]))
    chunk_decay = jnp.exp(log_gamma * C)
    qc = q.reshape(B, n, C, H, dk)
    kc = k.reshape(B, n, C, H, dk)
    vc = v.reshape(B, n, C, H, dv)
    scores = jnp.einsum('bnihd,bnjhd->bnhij', qc, kc) * decay_in
    o_inner = jnp.einsum('bnhij,bnjhe->bnihe', scores, vc)

    def step(state, xs):
        qn, kn, vn = xs
        cross = jnp.einsum('bihd,bhde->bihe', qn, state) * q_decay[None, :, :, None]
        state = state * chunk_decay[None, :, None, None] + jnp.einsum(
            'bjhd,bjhe->bhde', kn * k_decay[None, :, :, None], vn)
        return state, cross

    state0 = jnp.zeros((B, H, dk, dv), f32)
    _, o_cross = lax.scan(step, state0, (jnp.moveaxis(qc, 1, 0), jnp.moveaxis(kc, 1, 0), jnp.moveaxis(vc, 1, 0)))
    o = o_inner + jnp.moveaxis(o_cross, 0, 1)
    return o.reshape(B, S, H, dv)


def head_group_norm(o, w):
    mean = jnp.mean(o, axis=-1, keepdims=True)
    var = jnp.mean(jnp.square(o - mean), axis=-1, keepdims=True)
    return (o - mean) * lax.rsqrt(var + EPS) * w.astype(jnp.float32)


def causal_dwconv(u, w, b):
    S = u.shape[1]
    up = jnp.pad(u, ((0, 0), (CONV_WIDTH - 1, 0), (0, 0)))
    y = b
    for j in range(CONV_WIDTH):
        y = y + w[j] * up[:, j:j + S]
    return y


def setup_inputs(seed: int = 0) -> dict:
    key = jax.random.key(seed)
    ks = jax.random.split(key, 20)
    f32 = jnp.float32
    L = DEPTH

    def nrm(k, shape, fan_in):
        return jax.random.normal(k, shape, f32) * (fan_in ** -0.5)

    def gain(k, shape):
        return 1.0 + 0.02 * jax.random.normal(k, shape, f32)

    return {
        'x': jax.random.normal(ks[0], (BATCH, SEQ, D_MODEL), f32),
        'norm_mix_w': gain(ks[1], (L, D_MODEL)),
        'w_in': nrm(ks[2], (L, D_MODEL, IN_PROJ_W), D_MODEL),
        'cmp_pe_k': 0.02 * jax.random.normal(ks[3], (L, CMP_BLOCK, NSA_HEAD_DIM), f32),
        'cmp_w1_k': nrm(ks[4], (L, CMP_BLOCK * NSA_HEAD_DIM, CMP_HIDDEN), CMP_BLOCK * NSA_HEAD_DIM),
        'cmp_w2_k': nrm(ks[5], (L, CMP_HIDDEN, NSA_HEAD_DIM), CMP_HIDDEN),
        'cmp_pe_v': 0.02 * jax.random.normal(ks[6], (L, CMP_BLOCK, NSA_HEAD_DIM), f32),
        'cmp_w1_v': nrm(ks[7], (L, CMP_BLOCK * NSA_HEAD_DIM, CMP_HIDDEN), CMP_BLOCK * NSA_HEAD_DIM),
        'cmp_w2_v': nrm(ks[8], (L, CMP_HIDDEN, NSA_HEAD_DIM), CMP_HIDDEN),
        'w_nsa_branch': nrm(ks[9], (L, NSA_Q_W, D_MODEL), NSA_Q_W),
        'ret_gn_w': gain(ks[10], (L, RET_HEADS, RET_V_DIM)),
        'w_ret_branch': nrm(ks[11], (L, RET_V_W, D_MODEL), RET_V_W),
        'w_mix_out': nrm(ks[12], (L, D_MODEL, D_MODEL), D_MODEL),
        'norm_ffn_w': gain(ks[13], (L, D_MODEL)),
        'w_ffn_up': nrm(ks[14], (L, D_MODEL, 2 * D_FF), D_MODEL),
        'ffn_conv_w': nrm(ks[15], (L, CONV_WIDTH, 2 * D_FF), CONV_WIDTH),
        'ffn_conv_b': 0.01 * jax.random.normal(ks[16], (L, 2 * D_FF), f32),
        'w_ffn_down': nrm(ks[17], (L, D_FF, D_MODEL), D_FF),
        'norm_final_w': gain(ks[18], (D_MODEL,)),
    }


def reference(x, norm_mix_w, w_in, cmp_pe_k, cmp_w1_k, cmp_w2_k, cmp_pe_v, cmp_w1_v, cmp_w2_v,
              w_nsa_branch, ret_gn_w, w_ret_branch, w_mix_out, norm_ffn_w, w_ffn_up, ffn_conv_w,
              ffn_conv_b, w_ffn_down, norm_final_w):
    B, S, _ = x.shape
    split_at = [int(v) for v in np.cumsum(IN_SPLITS)[:-1]]
    for layer in range(DEPTH):
        h = rms_norm(x, norm_mix_w[layer])
        proj = h @ w_in[layer]
        q_nsa, kv_nsa, g_nsa, q_ret, k_ret, v_ret, g_ret, g_merge = jnp.split(proj, split_at, axis=-1)

        q_nsa = rope(q_nsa.reshape(B, S, NSA_HEADS, NSA_HEAD_DIM), ROPE_DIM, ROPE_THETA)
        kv = kv_nsa.reshape(B, S, 6, NSA_KV_GROUPS, NSA_HEAD_DIM)
        k_cmp, v_cmp = kv[:, :, 0], kv[:, :, 1]
        k_slc = rope(kv[:, :, 2], ROPE_DIM, ROPE_THETA)
        v_slc = kv[:, :, 3]
        k_win = rope(kv[:, :, 4], ROPE_DIM, ROPE_THETA)
        v_win = kv[:, :, 5]
        nsa_gates = jax.nn.sigmoid(g_nsa.reshape(B, S, NSA_HEADS, 3))
        o_a = nsa_attention(q_nsa, k_cmp, v_cmp, k_slc, v_slc, k_win, v_win, nsa_gates,
                            cmp_pe_k[layer], cmp_w1_k[layer], cmp_w2_k[layer],
                            cmp_pe_v[layer], cmp_w1_v[layer], cmp_w2_v[layer])
        y_a = o_a @ w_nsa_branch[layer]

        o_r = retention(q_ret.reshape(B, S, RET_HEADS, RET_QK_DIM),
                        k_ret.reshape(B, S, RET_HEADS, RET_QK_DIM),
                        v_ret.reshape(B, S, RET_HEADS, RET_V_DIM))
        o_r = head_group_norm(o_r, ret_gn_w[layer]) * jax.nn.silu(
            g_ret.reshape(B, S, RET_HEADS, RET_V_DIM).astype(jnp.float32))
        y_b = o_r.astype(x.dtype).reshape(B, S, RET_V_W) @ w_ret_branch[layer]

        gate_a, gate_b = jnp.split(jax.nn.sigmoid(g_merge), 2, axis=-1)
        x = x + (gate_a * y_a + gate_b * y_b) @ w_mix_out[layer]

        h = rms_norm(x, norm_ffn_w[layer])
        u = causal_dwconv(h @ w_ffn_up[layer], ffn_conv_w[layer], ffn_conv_b[layer])
        u_gate, u_val = jnp.split(u, 2, axis=-1)
        x = x + (jax.nn.silu(u_gate) * u_val) @ w_ffn_down[layer]
    return rms_norm(x, norm_final_w)
```

```python
import functools

import numpy as np
import jax
import jax.numpy as jnp
from jax import lax
from jax.experimental import pallas as pl
from jax.experimental.pallas import tpu as pltpu

F32 = jnp.float32
BF16 = jnp.bfloat16

NSA_HEADS = 8
NSA_KV = 2
NSA_G = NSA_HEADS // NSA_KV
DH = 64
ROPE_DIM = DH // 4
ROPE_THETA = 500000.0
CMP_BLOCK = 32
CMP_STRIDE = 16
CMP_HIDDEN = 256
SLC_BLOCK = 64
SLC_TOPK = 16
WINDOW = 512
RET_HEADS = 4
RET_DK = 128
RET_DV = 256
RET_THETA = 10000.0
CONV_WIDTH = 3
EPS = 1e-6
NEG = -1e30

LANES = 128
VMEM_LIMIT = 56 * 1024 * 1024

PROJ_TM = 256
ATT_SQ = 128
RET_C = 256
MIX_TM = 512
FFN_TM = 512
FFN_TF = 256
HALO = 8


def _cparams(sem):
    return pltpu.CompilerParams(dimension_semantics=sem, vmem_limit_bytes=VMEM_LIMIT)


def _rms(x, w):
    return x * lax.rsqrt(jnp.mean(x * x, axis=-1, keepdims=True) + EPS) * w


_C_Q = (0, 512)
_C_KVC = (512, 768)
_C_KSLC = (768, 1024)
_C_KWIN = (1024, 1280)
_C_VSW = (1280, 1536)
_C_G = (1536, 1664)
_C_QR = (1664, 2176)
_C_KR = (2176, 2688)
_C_VR = (2688, 3712)
_C_GR = (3712, 4736)
_PROJ_COLS = 4736


def _proj_kernel(x_ref, nw_ref, w_ref, cn_ref, s1_ref, s2_ref, cr_ref, sr_ref,
                 qT_ref, kvc_ref, kslc_ref, kwin_ref, vsT_ref, vwT_ref, gT_ref,
                 qr_ref, krT_ref, vr_ref, gr_ref, *, seq):
    tm = x_ref.shape[1]
    i = pl.program_id(1)
    hb = _rms(x_ref[0], nw_ref[...]).astype(BF16)

    def mm(c):
        return jnp.dot(hb, w_ref[:, c[0]:c[1]], preferred_element_type=F32)

    cn, s1, s2 = cn_ref[...], s1_ref[...], s2_ref[...]

    def rope_nsa(t):
        return t * cn + pltpu.roll(t, 8, 1) * s1 + pltpu.roll(t, LANES - 8, 1) * s2

    acc = mm(_C_Q)
    for j in range(4):
        t = rope_nsa(acc[:, j * LANES:(j + 1) * LANES]) * (DH ** -0.5)
        qT_ref[0, 2 * j:2 * j + 2] = t.T.astype(BF16).reshape(2, DH, tm)

    acc = mm(_C_KVC)
    for kind in range(2):
        kvc_ref[0, kind] = acc[:, kind * LANES:(kind + 1) * LANES]

    lane = lax.broadcasted_iota(jnp.int32, (tm, LANES), 1)
    pos = i * tm + lax.broadcasted_iota(jnp.int32, (tm, LANES), 0)
    onehot = jnp.where(lane - DH == pos // SLC_BLOCK, 1.0, 0.0)
    acc = mm(_C_KSLC)
    for h in range(NSA_KV):
        t = rope_nsa(acc[:, h * LANES:(h + 1) * LANES])
        kslc_ref[0, h] = jnp.where(lane >= DH, onehot, t).astype(BF16)
    acc = mm(_C_KWIN)
    for h in range(NSA_KV):
        kwin_ref[0, h] = rope_nsa(acc[:, h * LANES:(h + 1) * LANES]).astype(BF16)

    acc = mm(_C_VSW)
    vsT_ref[0] = acc[:, :LANES].T.astype(BF16).reshape(NSA_KV, DH, tm)
    vwT_ref[0] = acc[:, LANES:].T.astype(BF16).reshape(NSA_KV, DH, tm)

    g = jax.nn.sigmoid(mm(_C_G)).T
    gT_ref[0] = g[:32].reshape(NSA_KV, 16, tm)

    cr, sr = cr_ref[...], sr_ref[...]

    def rope_ret(t):
        return t * cr + pltpu.roll(t, RET_DK // 2, 1) * sr

    acc = mm(_C_QR)
    for j in range(RET_HEADS):
        qr_ref[0, :, j * LANES:(j + 1) * LANES] = rope_ret(acc[:, j * LANES:(j + 1) * LANES]).astype(BF16)
    acc = mm(_C_KR)
    for j in range(RET_HEADS):
        t = rope_ret(acc[:, j * LANES:(j + 1) * LANES]) * (RET_DK ** -0.5)
        krT_ref[0, j] = t.T.astype(BF16)
    vr_ref[0] = mm(_C_VR).astype(BF16)
    gr_ref[0] = mm(_C_GR).astype(BF16)


def _pack_proj_weight(w_in):
    w = w_in.astype(BF16)
    d = w.shape[0]
    z64 = jnp.zeros((d, DH), BF16)
    kv0 = NSA_HEADS * DH

    def kv(r, h):
        c = kv0 + r * NSA_KV * DH + h * DH
        return w[:, c:c + DH]

    g0 = kv0 + 6 * NSA_KV * DH
    gidx = []
    for h in range(NSA_KV):
        for r in range(3):
            for g in range(NSA_G):
                gidx.append(g0 + (h * NSA_G + g) * 3 + r)
        gidx += [g0] * 4
    gcols = w[:, np.asarray(gidx)]
    r0 = g0 + 3 * NSA_HEADS
    parts = [
        w[:, :kv0],
        w[:, kv0:kv0 + 2 * NSA_KV * DH],
        kv(2, 0), z64, kv(2, 1), z64,
        kv(4, 0), z64, kv(4, 1), z64,
        kv(3, 0), kv(3, 1), kv(5, 0), kv(5, 1),
        gcols, jnp.zeros((d, LANES - 32), BF16),
        w[:, r0:r0 + 2 * RET_HEADS * RET_DK + 2 * RET_HEADS * RET_DV],
    ]
    out = jnp.concatenate(parts, axis=1)
    assert out.shape[1] == _PROJ_COLS
    return out


def _rope_tables(seq):
    pos = jnp.arange(seq, dtype=F32)[:, None]
    half = ROPE_DIM // 2
    inv = jnp.power(jnp.float32(ROPE_THETA), -jnp.arange(half, dtype=F32) / half)
    ang = pos * inv[None, :]
    cos, sin = jnp.cos(ang), jnp.sin(ang)
    ones = jnp.ones((seq, DH - ROPE_DIM), F32)
    zeros = jnp.zeros((seq, DH - ROPE_DIM), F32)
    zh = jnp.zeros((seq, half), F32)
    cn = jnp.concatenate([cos, cos, ones], axis=1)
    s1 = jnp.concatenate([zh, sin, zeros], axis=1)
    s2 = jnp.concatenate([-sin, zh, zeros], axis=1)
    tile2 = lambda t: jnp.concatenate([t, t], axis=1)
    half_r = RET_DK // 2
    inv_r = jnp.power(jnp.float32(RET_THETA), -jnp.arange(half_r, dtype=F32) / half_r)
    ang_r = pos * inv_r[None, :]
    cos_r, sin_r = jnp.cos(ang_r), jnp.sin(ang_r)
    cr = jnp.concatenate([cos_r, cos_r], axis=1)
    sr = jnp.concatenate([-sin_r, sin_r], axis=1)
    return tile2(cn), tile2(s1), tile2(s2), cr, sr


def _projection(x, norm_w, w_in):
    B, S, D = x.shape
    tm = PROJ_TM
    wp = _pack_proj_weight(w_in)
    tabs = _rope_tables(S)
    row = lambda b, i: (b, i, 0)
    tab_spec = pl.BlockSpec((tm, LANES), lambda b, i: (i, 0))
    out_shape = (
        jax.ShapeDtypeStruct((B, NSA_HEADS, DH, S), BF16),
        jax.ShapeDtypeStruct((B, 2, S, NSA_KV * DH), F32),
        jax.ShapeDtypeStruct((B, NSA_KV, S, LANES), BF16),
        jax.ShapeDtypeStruct((B, NSA_KV, S, LANES), BF16),
        jax.ShapeDtypeStruct((B, NSA_KV, DH, S), BF16),
        jax.ShapeDtypeStruct((B, NSA_KV, DH, S), BF16),
        jax.ShapeDtypeStruct((B, NSA_KV, 16, S), F32),
        jax.ShapeDtypeStruct((B, S, RET_HEADS * RET_DK), BF16),
        jax.ShapeDtypeStruct((B, RET_HEADS, RET_DK, S), BF16),
        jax.ShapeDtypeStruct((B, S, RET_HEADS * RET_DV), BF16),
        jax.ShapeDtypeStruct((B, S, RET_HEADS * RET_DV), BF16),
    )
    tok_last = lambda n, d: pl.BlockSpec((1, n, d, tm), lambda b, i: (b, 0, 0, i))
    out_specs = (
        tok_last(NSA_HEADS, DH),
        pl.BlockSpec((1, 2, tm, NSA_KV * DH), lambda b, i: (b, 0, i, 0)),
        pl.BlockSpec((1, NSA_KV, tm, LANES), lambda b, i: (b, 0, i, 0)),
        pl.BlockSpec((1, NSA_KV, tm, LANES), lambda b, i: (b, 0, i, 0)),
        tok_last(NSA_KV, DH),
        tok_last(NSA_KV, DH),
        tok_last(NSA_KV, 16),
        pl.BlockSpec((1, tm, RET_HEADS * RET_DK), row),
        tok_last(RET_HEADS, RET_DK),
        pl.BlockSpec((1, tm, RET_HEADS * RET_DV), row),
        pl.BlockSpec((1, tm, RET_HEADS * RET_DV), row),
    )
    return pl.pallas_call(
        functools.partial(_proj_kernel, seq=S),
        grid=(B, S // tm),
        in_specs=[
            pl.BlockSpec((1, tm, D), row),
            pl.BlockSpec((1, D), lambda b, i: (0, 0)),
            pl.BlockSpec((D, _PROJ_COLS), lambda b, i: (0, 0)),
            tab_spec, tab_spec, tab_spec, tab_spec, tab_spec,
        ],
        out_specs=out_specs,
        out_shape=out_shape,
        compiler_params=_cparams(("arbitrary", "arbitrary")),
        name="in_proj",
    )(x, norm_w.reshape(1, D), wp, *tabs)


def _cmp_kernel(kvc_ref, pek_ref, pev_ref, w1k_ref, w1v_ref, w2k_ref, w2v_ref, kc_ref, vcT_ref):
    nseg = kvc_ref.shape[2] // CMP_STRIDE
    half = CMP_BLOCK // 2
    row = lax.broadcasted_iota(jnp.int32, (nseg, 1), 0)
    for kind, (pe_ref, w1_ref, w2_ref) in enumerate(((pek_ref, w1k_ref, w2k_ref), (pev_ref, w1v_ref, w2v_ref))):
        acc_a = [jnp.zeros((nseg, CMP_HIDDEN), F32) for _ in range(NSA_KV)]
        acc_b = [jnp.zeros((nseg, CMP_HIDDEN), F32) for _ in range(NSA_KV)]
        for l in range(half):
            xl = kvc_ref[0, kind, pl.ds(l, nseg, stride=CMP_STRIDE), :]
            xa = (xl + pe_ref[l:l + 1, :]).astype(BF16)
            xb = (xl + pe_ref[half + l:half + l + 1, :]).astype(BF16)
            for h in range(NSA_KV):
                acc_a[h] += jnp.dot(xa[:, h * DH:(h + 1) * DH], w1_ref[l], preferred_element_type=F32)
                acc_b[h] += jnp.dot(xb[:, h * DH:(h + 1) * DH], w1_ref[half + l], preferred_element_type=F32)
        for h in range(NSA_KV):
            nxt = jnp.concatenate([acc_b[h][1:], jnp.zeros((1, CMP_HIDDEN), F32)], axis=0)
            hid = jax.nn.gelu(acc_a[h] + nxt).astype(BF16)
            out = jnp.dot(hid, w2_ref[...], preferred_element_type=F32)
            out = jnp.where(row < nseg - 1, out, 0.0)
            if kind == 0:
                kc_ref[0, h] = jnp.concatenate([out, jnp.zeros((nseg, LANES - DH), F32)], axis=1).astype(BF16)
            else:
                vcT_ref[0, h] = out.T.astype(BF16)


def _compress(kvc, pe_k, w1_k, w2_k, pe_v, w1_v, w2_v):
    B, _, S, _ = kvc.shape
    nseg = S // CMP_STRIDE
    tile2 = lambda t: jnp.concatenate([t, t], axis=1)
    w1 = lambda w: w.astype(BF16).reshape(CMP_BLOCK, DH, CMP_HIDDEN)
    full = lambda *shape: pl.BlockSpec(shape, lambda b: (0,) * len(shape))
    return pl.pallas_call(
        _cmp_kernel,
        grid=(B,),
        in_specs=[
            pl.BlockSpec((1, 2, S, NSA_KV * DH), lambda b: (b, 0, 0, 0)),
            full(CMP_BLOCK, LANES), full(CMP_BLOCK, LANES),
            full(CMP_BLOCK, DH, CMP_HIDDEN), full(CMP_BLOCK, DH, CMP_HIDDEN),
            full(CMP_HIDDEN, DH), full(CMP_HIDDEN, DH),
        ],
        out_specs=(
            pl.BlockSpec((1, NSA_KV, nseg, LANES), lambda b: (b, 0, 0, 0)),
            pl.BlockSpec((1, NSA_KV, DH, nseg), lambda b: (b, 0, 0, 0)),
        ),
        out_shape=(
            jax.ShapeDtypeStruct((B, NSA_KV, nseg, LANES), BF16),
            jax.ShapeDtypeStruct((B, NSA_KV, DH, nseg), BF16),
        ),
        compiler_params=_cparams(("arbitrary",)),
        name="nsa_compress",
    )(kvc, tile2(pe_k), tile2(pe_v), w1(w1_k), w1(w1_v), w2_k.astype(BF16), w2_v.astype(BF16))


def _overlap_matrix(seq):
    nseg = seq // CMP_STRIDE
    n_slc = seq // SLC_BLOCK
    cs = np.arange(nseg) * CMP_STRIDE
    ss = np.arange(n_slc) * SLC_BLOCK
    ov = np.clip(np.minimum(cs[None, :] + CMP_BLOCK, ss[:, None] + SLC_BLOCK)
                 - np.maximum(cs[None, :], ss[:, None]), 0, None) / CMP_BLOCK
    ov[:, nseg - 1] = 0.0
    return ov


def _softmax_cols(s):
    m = jnp.max(s, axis=0, keepdims=True)
    p = jnp.exp(s - m)
    return m, p, jnp.sum(p, axis=0, keepdims=True)


def _att_kernel(qT_ref, ka_ref, vsT_ref, kw_ref, vwT_ref, kc_ref, vcT_ref, g_ref, ov_ref, o_ref):
    sq = qT_ref.shape[3]
    nq4 = NSA_G * sq
    n_slc = ov_ref.shape[0]
    nseg = kc_ref.shape[2]
    i = pl.program_id(2)
    q0 = pl.multiple_of(i * sq, sq)

    qT4 = jnp.concatenate([qT_ref[0, g] for g in range(NSA_G)], axis=1)
    qw = jnp.concatenate([qT4, jnp.zeros((LANES - DH, nq4), BF16)], axis=0)
    tq = q0 + lax.broadcasted_iota(jnp.int32, (1, sq), 1)
    t4 = jnp.concatenate([tq] * NSA_G, axis=1)

    s = jnp.dot(kc_ref[0, 0], qw, preferred_element_type=F32)
    c = lax.broadcasted_iota(jnp.int32, (nseg, 1), 0)
    cmask = jnp.logical_and(c * CMP_STRIDE + (CMP_BLOCK - 1) <= t4, c < nseg - 1)
    s = jnp.where(cmask, s, NEG)
    m = jnp.max(s, axis=0, keepdims=True)
    p = jnp.where(cmask, jnp.exp(s - m), 0.0)
    l = jnp.sum(p, axis=0, keepdims=True)
    pn = p * jnp.where(l > 0.0, 1.0 / l, 0.0)
    o_cmp = jnp.dot(vcT_ref[0, 0], pn.astype(BF16), preferred_element_type=F32)

    psum = pn[:, 0:sq]
    for g in range(1, NSA_G):
        psum = psum + pn[:, g * sq:(g + 1) * sq]
    p_hi = psum.astype(BF16)
    p_lo = (psum - p_hi.astype(F32)).astype(BF16)
    imp = (jnp.dot(ov_ref[...], p_hi, preferred_element_type=F32)
           + jnp.dot(ov_ref[...], p_lo, preferred_element_type=F32))
    jb = lax.broadcasted_iota(jnp.int32, (n_slc, 1), 0)
    cur = tq // SLC_BLOCK
    forced = jnp.logical_or(jb == 0, jnp.logical_or(jb == cur, jb == cur - 1))
    imp = jnp.where(forced, jnp.inf, jnp.where(jb <= cur, imp, -jnp.inf))
    rank = jnp.zeros((n_slc, sq), F32)
    for jp in range(n_slc):
        r = imp[jp:jp + 1, :]
        later = jnp.where(jb > jp, 1.0, 0.0)
        rank = rank + jnp.where(r > imp, 1.0, 0.0) + jnp.where(r == imp, later, 0.0)
    bias = jnp.where(rank < float(min(SLC_TOPK, n_slc)), 0.0, NEG).astype(BF16)
    if n_slc < LANES - DH:
        bias = jnp.concatenate([bias, jnp.zeros((LANES - DH - n_slc, sq), BF16)], axis=0)
    qa = jnp.concatenate([qT4, jnp.concatenate([bias] * NSA_G, axis=1)], axis=0)

    kpos = q0 + lax.broadcasted_iota(jnp.int32, (sq, 1), 0)
    s = jnp.dot(ka_ref[0, 0, pl.ds(q0, sq), :], qa, preferred_element_type=F32)
    s = jnp.where(kpos <= t4, s, NEG)
    m, p, l = _softmax_cols(s)
    acc = jnp.dot(vsT_ref[0, 0, :, pl.ds(q0, sq)], p.astype(BF16), preferred_element_type=F32)

    def past(kt, carry):
        m, l, acc = carry
        k0 = pl.multiple_of(kt * sq, sq)
        s = jnp.dot(ka_ref[0, 0, pl.ds(k0, sq), :], qa, preferred_element_type=F32)
        m_new = jnp.maximum(m, jnp.max(s, axis=0, keepdims=True))
        alpha = jnp.exp(m - m_new)
        p = jnp.exp(s - m_new)
        l = alpha * l + jnp.sum(p, axis=0, keepdims=True)
        acc = alpha * acc + jnp.dot(vsT_ref[0, 0, :, pl.ds(k0, sq)], p.astype(BF16),
                                    preferred_element_type=F32)
        return m_new, l, acc

    m, l, acc = lax.fori_loop(0, i, past, (m, l, acc))
    o_slc = acc * (1.0 / l)

    nk = WINDOW + sq
    ks = pl.multiple_of(jnp.maximum(q0 - WINDOW, 0), sq)
    s = jnp.dot(kw_ref[0, 0, pl.ds(ks, nk), :], qw, preferred_element_type=F32)
    d = t4 - (ks + lax.broadcasted_iota(jnp.int32, (nk, 1), 0))
    s = jnp.where(jnp.logical_and(d >= 0, d < WINDOW), s, NEG)
    _, p, l = _softmax_cols(s)
    o_win = jnp.dot(vwT_ref[0, 0, :, pl.ds(ks, nk)], p.astype(BF16), preferred_element_type=F32) * (1.0 / l)

    def gate(r):
        return jnp.concatenate([g_ref[0, 0, r * NSA_G + g:r * NSA_G + g + 1, :] for g in range(NSA_G)], axis=1)

    o = gate(0) * o_cmp + gate(1) * o_slc + gate(2) * o_win
    o = jnp.concatenate([o[:, g * sq:(g + 1) * sq] for g in range(NSA_G)], axis=0)
    o_ref[0, 0] = o.T.astype(BF16)


def _attention(qT, kaug, vsT, kwin, vwT, kc, vcT, gT):
    B, _, _, S = qT.shape
    sq = ATT_SQ
    nseg = S // CMP_STRIDE
    n_slc = S // SLC_BLOCK
    ov = jnp.asarray(_overlap_matrix(S), BF16)
    per_bh = lambda *shape: pl.BlockSpec((1, 1) + shape, lambda b, h, i: (b, h, 0, 0))
    return pl.pallas_call(
        _att_kernel,
        grid=(B, NSA_KV, S // sq),
        in_specs=[
            pl.BlockSpec((1, NSA_G, DH, sq), lambda b, h, i: (b, h, 0, i)),
            per_bh(S, LANES), per_bh(DH, S), per_bh(S, LANES), per_bh(DH, S),
            per_bh(nseg, LANES), per_bh(DH, nseg),
            pl.BlockSpec((1, 1, 16, sq), lambda b, h, i: (b, h, 0, i)),
            pl.BlockSpec((n_slc, nseg), lambda b, h, i: (0, 0)),
        ],
        out_specs=pl.BlockSpec((1, 1, sq, NSA_G * DH), lambda b, h, i: (b, h, i, 0)),
        out_shape=jax.ShapeDtypeStruct((B, NSA_KV, S, NSA_G * DH), BF16),
        compiler_params=_cparams(("arbitrary", "arbitrary", "arbitrary")),
        name="nsa_attention",
    )(qT, kaug, vsT, kwin, vwT, kc, vcT, gT, ov)


def _ret_kernel(q_ref, kT_ref, v_ref, g_ref, din_ref, qd_ref, kd_ref, cd_ref, gn_ref, o_ref, state_ref):
    @pl.when(pl.program_id(2) == 0)
    def _():
        state_ref[...] = jnp.zeros_like(state_ref)

    q = q_ref[0]
    kT = kT_ref[0, 0]
    v = v_ref[0]
    state = state_ref[...]
    sc = jnp.dot(q, kT, preferred_element_type=F32) * din_ref[0]
    o = jnp.dot(sc.astype(BF16), v, preferred_element_type=F32)
    o = o + jnp.dot(q, state.astype(BF16), preferred_element_type=F32) * qd_ref[0]
    kd = (kT.astype(F32) * kd_ref[0]).astype(BF16)
    state_ref[...] = state * cd_ref[0] + jnp.dot(kd, v, preferred_element_type=F32)

    mean = jnp.mean(o, axis=-1, keepdims=True)
    var = jnp.mean(jnp.square(o - mean), axis=-1, keepdims=True)
    y = (o - mean) * lax.rsqrt(var + EPS) * gn_ref[0]
    o_ref[0] = (y * jax.nn.silu(g_ref[0].astype(F32))).astype(BF16)


def _retention(qr, krT, vr, gr, gn_w):
    B, S, _ = qr.shape
    C = RET_C
    H = RET_HEADS
    log_gamma = jnp.log1p(-jnp.exp2(-5.0 - jnp.arange(H, dtype=F32)))
    idx = jnp.arange(C, dtype=F32)
    diff = idx[:, None] - idx[None, :]
    din = jnp.where(diff >= 0, jnp.exp(log_gamma[:, None, None] * jnp.maximum(diff, 0.0)), 0.0)
    qd = jnp.exp(log_gamma[:, None] * (idx[None, :] + 1.0))
    qd = jnp.broadcast_to(qd[:, :, None], (H, C, RET_DV))
    kd = jnp.exp(log_gamma[:, None] * (C - 1.0 - idx[None, :]))[:, None, :]
    cd = jnp.broadcast_to(jnp.exp(log_gamma * C)[:, None, None], (H, 1, RET_DV))
    per_h = lambda *shape: pl.BlockSpec((1,) + shape, lambda b, h, n: (h, 0, 0))
    return pl.pallas_call(
        _ret_kernel,
        grid=(B, H, S // C),
        in_specs=[
            pl.BlockSpec((1, C, RET_DK), lambda b, h, n: (b, n, h)),
            pl.BlockSpec((1, 1, RET_DK, C), lambda b, h, n: (b, h, 0, n)),
            pl.BlockSpec((1, C, RET_DV), lambda b, h, n: (b, n, h)),
            pl.BlockSpec((1, C, RET_DV), lambda b, h, n: (b, n, h)),
            per_h(C, C), per_h(C, RET_DV), per_h(1, C), per_h(1, RET_DV), per_h(1, RET_DV),
        ],
        out_specs=pl.BlockSpec((1, C, RET_DV), lambda b, h, n: (b, n, h)),
        out_shape=jax.ShapeDtypeStruct((B, S, H * RET_DV), BF16),
        scratch_shapes=[pltpu.VMEM((RET_DK, RET_DV), F32)],
        compiler_params=_cparams(("arbitrary", "arbitrary", "arbitrary")),
        name="retention",
    )(qr, krT, vr, gr, din, qd, kd, cd, gn_w.reshape(H, 1, RET_DV))


def _mix_kernel(x_ref, nw_ref, oa_ref, or_ref, wg_ref, wn_ref, wr_ref, wm_ref, o_ref):
    d = x_ref.shape[2]
    x = x_ref[0]
    hb = _rms(x, nw_ref[...]).astype(BF16)
    gate = jax.nn.sigmoid(jnp.dot(hb, wg_ref[...], preferred_element_type=F32))
    kw = NSA_G * DH
    y_a = jnp.dot(oa_ref[0, 0], wn_ref[0:kw, :], preferred_element_type=F32)
    for h in range(1, NSA_KV):
        y_a = y_a + jnp.dot(oa_ref[0, h], wn_ref[h * kw:(h + 1) * kw, :], preferred_element_type=F32)
    y_b = jnp.dot(or_ref[0], wr_ref[...], preferred_element_type=F32)
    mix = (gate[:, :d] * y_a + gate[:, d:] * y_b).astype(BF16)
    o_ref[0] = x + jnp.dot(mix, wm_ref[...], preferred_element_type=F32)


def _mix(x, norm_w, oa, o_r, wg, wn, wr, wm):
    B, S, D = x.shape
    tm = MIX_TM
    row = lambda b, i: (b, i, 0)
    full = lambda a: pl.BlockSpec(a.shape, lambda b, i: (0,) * a.ndim)
    nw = norm_w.reshape(1, D)
    return pl.pallas_call(
        _mix_kernel,
        grid=(B, S // tm),
        in_specs=[
            pl.BlockSpec((1, tm, D), row), full(nw),
            pl.BlockSpec((1, NSA_KV, tm, NSA_G * DH), lambda b, i: (b, 0, i, 0)),
            pl.BlockSpec((1, tm, RET_HEADS * RET_DV), row),
            full(wg), full(wn), full(wr), full(wm),
        ],
        out_specs=pl.BlockSpec((1, tm, D), row),
        out_shape=jax.ShapeDtypeStruct((B, S, D), F32),
        compiler_params=_cparams(("arbitrary", "arbitrary")),
        name="mixer_merge",
    )(x, nw, oa, o_r, wg, wn, wr, wm)


def _ffn_kernel(x_ref, halo_ref, nw_ref, wu_ref, cw_ref, cb_ref, wd_ref, fw_ref, o_ref, acc_ref, *, final_norm):
    tm = x_ref.shape[1]
    tf = wd_ref.shape[1]
    x = x_ref[0]
    h_main = _rms(x, nw_ref[...])
    h_halo = jnp.where(pl.program_id(1) > 0, _rms(halo_ref[0], nw_ref[...]), 0.0)
    hb = jnp.concatenate([h_halo, h_main], axis=0).astype(BF16)
    acc_ref[...] = jnp.zeros_like(acc_ref)

    def step(j, carry):
        u = jnp.dot(hb, wu_ref[j], preferred_element_type=F32)
        cw = cw_ref[j]
        y = cb_ref[j]
        for k in range(CONV_WIDTH):
            lo = HALO - (CONV_WIDTH - 1) + k
            y = y + cw[k:k + 1, :] * u[lo:lo + tm, :]
        act = (jax.nn.silu(y[:, :tf]) * y[:, tf:]).astype(BF16)
        acc_ref[...] += jnp.dot(act, wd_ref[j], preferred_element_type=F32)
        return carry

    lax.fori_loop(0, wu_ref.shape[0], step, 0)
    y = x + acc_ref[...]
    o_ref[0] = _rms(y, fw_ref[...]) if final_norm else y


def _ffn(x, norm_w, w_up, conv_w, conv_b, w_down, final_w, final_norm):
    B, S, D = x.shape
    tm, tf = FFN_TM, FFN_TF
    dff = w_down.shape[0]
    nt = dff // tf
    pair = lambda a: jnp.concatenate([a[..., :dff].reshape(a.shape[:-1] + (nt, tf)),
                                      a[..., dff:].reshape(a.shape[:-1] + (nt, tf))], axis=-1)
    wu = jnp.swapaxes(pair(w_up.astype(BF16)), 0, 1)
    cw = jnp.swapaxes(pair(conv_w), 0, 1)
    cb = pair(conv_b)[:, None, :]
    wd = w_down.astype(BF16).reshape(nt, tf, D)
    row = lambda b, i: (b, i, 0)
    full = lambda a: pl.BlockSpec(a.shape, lambda b, i: (0,) * a.ndim)
    nw = norm_w.reshape(1, D)
    fw = final_w.reshape(1, D)
    return pl.pallas_call(
        functools.partial(_ffn_kernel, final_norm=final_norm),
        grid=(B, S // tm),
        in_specs=[
            pl.BlockSpec((1, tm, D), row),
            pl.BlockSpec((1, HALO, D), lambda b, i: (b, jnp.maximum(i * (tm // HALO) - 1, 0), 0)),
            full(nw), full(wu), full(cw), full(cb), full(wd), full(fw),
        ],
        out_specs=pl.BlockSpec((1, tm, D), row),
        out_shape=jax.ShapeDtypeStruct((B, S, D), F32),
        scratch_shapes=[pltpu.VMEM((tm, D), F32)],
        compiler_params=_cparams(("arbitrary", "arbitrary")),
        name="conv_ffn",
    )(x, x, nw, wu, cw, cb, wd, fw)


def kernel(x, norm_mix_w, w_in, cmp_pe_k, cmp_w1_k, cmp_w2_k, cmp_pe_v, cmp_w1_v, cmp_w2_v, w_nsa_branch, ret_gn_w, w_ret_branch, w_mix_out, norm_ffn_w, w_ffn_up, ffn_conv_w, ffn_conv_b, w_ffn_down, norm_final_w):
    depth = w_in.shape[0]
    d = x.shape[-1]
    for layer in range(depth):
        qT, kvc, kaug, kwin, vsT, vwT, gT, qr, krT, vr, gr = _projection(x, norm_mix_w[layer], w_in[layer])
        kc, vcT = _compress(kvc, cmp_pe_k[layer], cmp_w1_k[layer], cmp_w2_k[layer],
                            cmp_pe_v[layer], cmp_w1_v[layer], cmp_w2_v[layer])
        oa = _attention(qT, kaug, vsT, kwin, vwT, kc, vcT, gT)
        o_r = _retention(qr, krT, vr, gr, ret_gn_w[layer])
        wg = w_in[layer][:, -2 * d:].astype(BF16)
        x1 = _mix(x, norm_mix_w[layer], oa, o_r, wg, w_nsa_branch[layer].astype(BF16),
                  w_ret_branch[layer].astype(BF16), w_mix_out[layer].astype(BF16))
        x = _ffn(x1, norm_ffn_w[layer], w_ffn_up[layer], ffn_conv_w[layer], ffn_conv_b[layer],
                 w_ffn_down[layer], norm_final_w, final_norm=layer == depth - 1)
    return x
```

```python
import functools

import numpy as np
import jax
import jax.numpy as jnp
from jax import lax
from jax.experimental import pallas as pl
from jax.experimental.pallas import tpu as pltpu

F32 = jnp.float32
BF16 = jnp.bfloat16

NSA_HEADS = 8
NSA_KV = 2
NSA_G = NSA_HEADS // NSA_KV
DH = 64
ROPE_DIM = DH // 4
ROPE_THETA = 500000.0
CMP_BLOCK = 32
CMP_STRIDE = 16
CMP_HIDDEN = 256
SLC_BLOCK = 64
SLC_TOPK = 16
WINDOW = 512
RET_HEADS = 4
RET_DK = 128
RET_DV = 256
RET_THETA = 10000.0
CONV_WIDTH = 3
EPS = 1e-6
NEG = -1e30

LANES = 128
VMEM_LIMIT = 56 * 1024 * 1024

PROJ_TM = 256
ATT_SQ = 128
ATT_CH = 512
RET_C = 256
MIX_TM = 512
FFN_TM = 512
FFN_TF = 256
HALO = 8


def _cparams(sem):
    return pltpu.CompilerParams(dimension_semantics=sem, vmem_limit_bytes=VMEM_LIMIT)


def _rms(x, w):
    return x * lax.rsqrt(jnp.mean(x * x, axis=-1, keepdims=True) + EPS) * w


_C_Q = (0, 512)
_C_KVC = (512, 768)
_C_KSLC = (768, 1024)
_C_KWIN = (1024, 1280)
_C_VSW = (1280, 1536)
_C_G = (1536, 1664)
_C_QR = (1664, 2176)
_C_KR = (2176, 2688)
_C_VR = (2688, 3712)
_C_GR = (3712, 4736)
_PROJ_COLS = 4736


def _proj_kernel(x_ref, nw_ref, w_ref, cn_ref, s1_ref, s2_ref, cr_ref, sr_ref,
                 qT_ref, kvc_ref, kslc_ref, kwin_ref, vsT_ref, vwT_ref, gT_ref,
                 qr_ref, krT_ref, vr_ref, gr_ref, *, seq):
    tm = x_ref.shape[1]
    i = pl.program_id(1)
    hb = _rms(x_ref[0], nw_ref[...]).astype(BF16)

    def mm(c):
        return jnp.dot(hb, w_ref[:, c[0]:c[1]], preferred_element_type=F32)

    cn, s1, s2 = cn_ref[...], s1_ref[...], s2_ref[...]

    def rope_nsa(t):
        return t * cn + pltpu.roll(t, 8, 1) * s1 + pltpu.roll(t, LANES - 8, 1) * s2

    acc = mm(_C_Q)
    for j in range(4):
        t = rope_nsa(acc[:, j * LANES:(j + 1) * LANES]) * (DH ** -0.5)
        qT_ref[0, 2 * j:2 * j + 2] = t.T.astype(BF16).reshape(2, DH, tm)

    acc = mm(_C_KVC)
    for kind in range(2):
        kvc_ref[0, kind] = acc[:, kind * LANES:(kind + 1) * LANES]

    lane = lax.broadcasted_iota(jnp.int32, (tm, LANES), 1)
    pos = i * tm + lax.broadcasted_iota(jnp.int32, (tm, LANES), 0)
    onehot = jnp.where(lane - DH == pos // SLC_BLOCK, 1.0, 0.0)
    acc = mm(_C_KSLC)
    for h in range(NSA_KV):
        t = rope_nsa(acc[:, h * LANES:(h + 1) * LANES])
        kslc_ref[0, h] = jnp.where(lane >= DH, onehot, t).astype(BF16)
    acc = mm(_C_KWIN)
    for h in range(NSA_KV):
        kwin_ref[0, h] = rope_nsa(acc[:, h * LANES:(h + 1) * LANES]).astype(BF16)

    acc = mm(_C_VSW)
    vsT_ref[0] = acc[:, :LANES].T.astype(BF16).reshape(NSA_KV, DH, tm)
    vwT_ref[0] = acc[:, LANES:].T.astype(BF16).reshape(NSA_KV, DH, tm)

    g = jax.nn.sigmoid(mm(_C_G)).T
    gT_ref[0] = g[:32].reshape(NSA_KV, 16, tm)

    cr, sr = cr_ref[...], sr_ref[...]

    def rope_ret(t):
        return t * cr + pltpu.roll(t, RET_DK // 2, 1) * sr

    acc = mm(_C_QR)
    for j in range(RET_HEADS):
        qr_ref[0, :, j * LANES:(j + 1) * LANES] = rope_ret(acc[:, j * LANES:(j + 1) * LANES]).astype(BF16)
    acc = mm(_C_KR)
    for j in range(RET_HEADS):
        t = rope_ret(acc[:, j * LANES:(j + 1) * LANES]) * (RET_DK ** -0.5)
        krT_ref[0, j] = t.T.astype(BF16)
    vr_ref[0] = mm(_C_VR).astype(BF16)
    gr_ref[0] = mm(_C_GR).astype(BF16)


def _pack_proj_weight(w_in):
    w = w_in.astype(BF16)
    d = w.shape[0]
    z64 = jnp.zeros((d, DH), BF16)
    kv0 = NSA_HEADS * DH

    def kv(r, h):
        c = kv0 + r * NSA_KV * DH + h * DH
        return w[:, c:c + DH]

    g0 = kv0 + 6 * NSA_KV * DH
    gidx = []
    for h in range(NSA_KV):
        for r in range(3):
            for g in range(NSA_G):
                gidx.append(g0 + (h * NSA_G + g) * 3 + r)
        gidx += [g0] * 4
    gcols = w[:, np.asarray(gidx)]
    r0 = g0 + 3 * NSA_HEADS
    parts = [
        w[:, :kv0],
        w[:, kv0:kv0 + 2 * NSA_KV * DH],
        kv(2, 0), z64, kv(2, 1), z64,
        kv(4, 0), z64, kv(4, 1), z64,
        kv(3, 0), kv(3, 1), kv(5, 0), kv(5, 1),
        gcols, jnp.zeros((d, LANES - 32), BF16),
        w[:, r0:r0 + 2 * RET_HEADS * RET_DK + 2 * RET_HEADS * RET_DV],
    ]
    out = jnp.concatenate(parts, axis=1)
    assert out.shape[1] == _PROJ_COLS
    return out


def _rope_tables(seq):
    pos = jnp.arange(seq, dtype=F32)[:, None]
    half = ROPE_DIM // 2
    inv = jnp.power(jnp.float32(ROPE_THETA), -jnp.arange(half, dtype=F32) / half)
    ang = pos * inv[None, :]
    cos, sin = jnp.cos(ang), jnp.sin(ang)
    ones = jnp.ones((seq, DH - ROPE_DIM), F32)
    zeros = jnp.zeros((seq, DH - ROPE_DIM), F32)
    zh = jnp.zeros((seq, half), F32)
    cn = jnp.concatenate([cos, cos, ones], axis=1)
    s1 = jnp.concatenate([zh, sin, zeros], axis=1)
    s2 = jnp.concatenate([-sin, zh, zeros], axis=1)
    tile2 = lambda t: jnp.concatenate([t, t], axis=1)
    half_r = RET_DK // 2
    inv_r = jnp.power(jnp.float32(RET_THETA), -jnp.arange(half_r, dtype=F32) / half_r)
    ang_r = pos * inv_r[None, :]
    cos_r, sin_r = jnp.cos(ang_r), jnp.sin(ang_r)
    cr = jnp.concatenate([cos_r, cos_r], axis=1)
    sr = jnp.concatenate([-sin_r, sin_r], axis=1)
    return tile2(cn), tile2(s1), tile2(s2), cr, sr


def _projection(x, norm_w, w_in):
    B, S, D = x.shape
    tm = PROJ_TM
    wp = _pack_proj_weight(w_in)
    tabs = _rope_tables(S)
    row = lambda b, i: (b, i, 0)
    tab_spec = pl.BlockSpec((tm, LANES), lambda b, i: (i, 0))
    out_shape = (
        jax.ShapeDtypeStruct((B, NSA_HEADS, DH, S), BF16),
        jax.ShapeDtypeStruct((B, 2, S, NSA_KV * DH), F32),
        jax.ShapeDtypeStruct((B, NSA_KV, S, LANES), BF16),
        jax.ShapeDtypeStruct((B, NSA_KV, S, LANES), BF16),
        jax.ShapeDtypeStruct((B, NSA_KV, DH, S), BF16),
        jax.ShapeDtypeStruct((B, NSA_KV, DH, S), BF16),
        jax.ShapeDtypeStruct((B, NSA_KV, 16, S), F32),
        jax.ShapeDtypeStruct((B, S, RET_HEADS * RET_DK), BF16),
        jax.ShapeDtypeStruct((B, RET_HEADS, RET_DK, S), BF16),
        jax.ShapeDtypeStruct((B, S, RET_HEADS * RET_DV), BF16),
        jax.ShapeDtypeStruct((B, S, RET_HEADS * RET_DV), BF16),
    )
    tok_last = lambda n, d: pl.BlockSpec((1, n, d, tm), lambda b, i: (b, 0, 0, i))
    out_specs = (
        tok_last(NSA_HEADS, DH),
        pl.BlockSpec((1, 2, tm, NSA_KV * DH), lambda b, i: (b, 0, i, 0)),
        pl.BlockSpec((1, NSA_KV, tm, LANES), lambda b, i: (b, 0, i, 0)),
        pl.BlockSpec((1, NSA_KV, tm, LANES), lambda b, i: (b, 0, i, 0)),
        tok_last(NSA_KV, DH),
        tok_last(NSA_KV, DH),
        tok_last(NSA_KV, 16),
        pl.BlockSpec((1, tm, RET_HEADS * RET_DK), row),
        tok_last(RET_HEADS, RET_DK),
        pl.BlockSpec((1, tm, RET_HEADS * RET_DV), row),
        pl.BlockSpec((1, tm, RET_HEADS * RET_DV), row),
    )
    return pl.pallas_call(
        functools.partial(_proj_kernel, seq=S),
        grid=(B, S // tm),
        in_specs=[
            pl.BlockSpec((1, tm, D), row),
            pl.BlockSpec((1, D), lambda b, i: (0, 0)),
            pl.BlockSpec((D, _PROJ_COLS), lambda b, i: (0, 0)),
            tab_spec, tab_spec, tab_spec, tab_spec, tab_spec,
        ],
        out_specs=out_specs,
        out_shape=out_shape,
        compiler_params=_cparams(("arbitrary", "arbitrary")),
        name="in_proj",
    )(x, norm_w.reshape(1, D), wp, *tabs)


def _cmp_kernel(kvc_ref, pek_ref, pev_ref, w1k_ref, w1v_ref, w2k_ref, w2v_ref, kc_ref, vcT_ref):
    nseg = kvc_ref.shape[2] // CMP_STRIDE
    half = CMP_BLOCK // 2
    row = lax.broadcasted_iota(jnp.int32, (nseg, 1), 0)
    for kind, (pe_ref, w1_ref, w2_ref) in enumerate(((pek_ref, w1k_ref, w2k_ref), (pev_ref, w1v_ref, w2v_ref))):
        acc_a = [jnp.zeros((nseg, CMP_HIDDEN), F32) for _ in range(NSA_KV)]
        acc_b = [jnp.zeros((nseg, CMP_HIDDEN), F32) for _ in range(NSA_KV)]
        for l in range(half):
            xl = kvc_ref[0, kind, pl.ds(l, nseg, stride=CMP_STRIDE), :]
            xa = (xl + pe_ref[l:l + 1, :]).astype(BF16)
            xb = (xl + pe_ref[half + l:half + l + 1, :]).astype(BF16)
            for h in range(NSA_KV):
                acc_a[h] += jnp.dot(xa[:, h * DH:(h + 1) * DH], w1_ref[l], preferred_element_type=F32)
                acc_b[h] += jnp.dot(xb[:, h * DH:(h + 1) * DH], w1_ref[half + l], preferred_element_type=F32)
        for h in range(NSA_KV):
            nxt = jnp.concatenate([acc_b[h][1:], jnp.zeros((1, CMP_HIDDEN), F32)], axis=0)
            hid = jax.nn.gelu(acc_a[h] + nxt).astype(BF16)
            out = jnp.dot(hid, w2_ref[...], preferred_element_type=F32)
            out = jnp.where(row < nseg - 1, out, 0.0)
            if kind == 0:
                kc_ref[0, h] = jnp.concatenate([out, jnp.zeros((nseg, LANES - DH), F32)], axis=1).astype(BF16)
            else:
                vcT_ref[0, h] = out.T.astype(BF16)


def _compress(kvc, pe_k, w1_k, w2_k, pe_v, w1_v, w2_v):
    B, _, S, _ = kvc.shape
    nseg = S // CMP_STRIDE
    tile2 = lambda t: jnp.concatenate([t, t], axis=1)
    w1 = lambda w: w.astype(BF16).reshape(CMP_BLOCK, DH, CMP_HIDDEN)
    full = lambda *shape: pl.BlockSpec(shape, lambda b: (0,) * len(shape))
    return pl.pallas_call(
        _cmp_kernel,
        grid=(B,),
        in_specs=[
            pl.BlockSpec((1, 2, S, NSA_KV * DH), lambda b: (b, 0, 0, 0)),
            full(CMP_BLOCK, LANES), full(CMP_BLOCK, LANES),
            full(CMP_BLOCK, DH, CMP_HIDDEN), full(CMP_BLOCK, DH, CMP_HIDDEN),
            full(CMP_HIDDEN, DH), full(CMP_HIDDEN, DH),
        ],
        out_specs=(
            pl.BlockSpec((1, NSA_KV, nseg, LANES), lambda b: (b, 0, 0, 0)),
            pl.BlockSpec((1, NSA_KV, DH, nseg), lambda b: (b, 0, 0, 0)),
        ),
        out_shape=(
            jax.ShapeDtypeStruct((B, NSA_KV, nseg, LANES), BF16),
            jax.ShapeDtypeStruct((B, NSA_KV, DH, nseg), BF16),
        ),
        compiler_params=_cparams(("arbitrary",)),
        name="nsa_compress",
    )(kvc, tile2(pe_k), tile2(pe_v), w1(w1_k), w1(w1_v), w2_k.astype(BF16), w2_v.astype(BF16))


def _overlap_matrix(seq):
    nseg = seq // CMP_STRIDE
    n_slc = seq // SLC_BLOCK
    cs = np.arange(nseg) * CMP_STRIDE
    ss = np.arange(n_slc) * SLC_BLOCK
    ov = np.clip(np.minimum(cs[None, :] + CMP_BLOCK, ss[:, None] + SLC_BLOCK)
                 - np.maximum(cs[None, :], ss[:, None]), 0, None) / CMP_BLOCK
    ov[:, nseg - 1] = 0.0
    return ov


def _softmax_cols(s):
    m = jnp.max(s, axis=0, keepdims=True)
    p = jnp.exp(s - m)
    return m, p, jnp.sum(p, axis=0, keepdims=True)


def _att_kernel(qT_ref, ka_ref, vsT_ref, kw_ref, vwT_ref, kc_ref, vcT_ref, g_ref, ov_ref, o_ref, s_scr):
    sq = qT_ref.shape[3]
    nq4 = NSA_G * sq
    n_slc = ov_ref.shape[0]
    nseg = kc_ref.shape[2]
    i = pl.program_id(2)
    q0 = pl.multiple_of(i * sq, sq)

    qT4 = jnp.concatenate([qT_ref[0, g] for g in range(NSA_G)], axis=1)
    qw = jnp.concatenate([qT4, jnp.zeros((LANES - DH, nq4), BF16)], axis=0)
    tq = q0 + lax.broadcasted_iota(jnp.int32, (1, sq), 1)
    t4 = jnp.concatenate([tq] * NSA_G, axis=1)

    s = jnp.dot(kc_ref[0, 0], qw, preferred_element_type=F32)
    c = lax.broadcasted_iota(jnp.int32, (nseg, 1), 0)
    cmask = jnp.logical_and(c * CMP_STRIDE + (CMP_BLOCK - 1) <= t4, c < nseg - 1)
    s = jnp.where(cmask, s, NEG)
    m = jnp.max(s, axis=0, keepdims=True)
    p = jnp.where(cmask, jnp.exp(s - m), 0.0)
    l = jnp.sum(p, axis=0, keepdims=True)
    pn = p * jnp.where(l > 0.0, 1.0 / l, 0.0)
    o_cmp = jnp.dot(vcT_ref[0, 0], pn.astype(BF16), preferred_element_type=F32)

    psum = pn[:, 0:sq]
    for g in range(1, NSA_G):
        psum = psum + pn[:, g * sq:(g + 1) * sq]
    p_hi = psum.astype(BF16)
    p_lo = (psum - p_hi.astype(F32)).astype(BF16)
    imp = (jnp.dot(ov_ref[...], p_hi, preferred_element_type=F32)
           + jnp.dot(ov_ref[...], p_lo, preferred_element_type=F32))
    jb = lax.broadcasted_iota(jnp.int32, (n_slc, 1), 0)
    cur = tq // SLC_BLOCK
    forced = jnp.logical_or(jb == 0, jnp.logical_or(jb == cur, jb == cur - 1))
    imp = jnp.where(forced, jnp.inf, jnp.where(jb <= cur, imp, -jnp.inf))
    sub = lax.broadcasted_iota(jnp.int32, (8, 1), 0)
    groups = [imp[8 * v:8 * v + 8] for v in range(n_slc // 8)]
    ranks = [jnp.zeros((8, sq), F32) for _ in groups]
    for jp in range(n_slc):
        r = imp[jp:jp + 1, :]
        for v, x in enumerate(groups):
            ge = jnp.where(r >= x, 1.0, 0.0)
            gt = jnp.where(r > x, 1.0, 0.0)
            if jp < 8 * v:
                cnt = ge
            elif jp >= 8 * v + 8:
                cnt = gt
            else:
                cnt = jnp.where(sub > jp - 8 * v, ge, gt)
            ranks[v] = ranks[v] + cnt
    rank = jnp.concatenate(ranks, axis=0)
    bias = jnp.where(rank < float(min(SLC_TOPK, n_slc)), 0.0, NEG).astype(BF16)
    if n_slc < LANES - DH:
        bias = jnp.concatenate([bias, jnp.zeros((LANES - DH - n_slc, sq), BF16)], axis=0)
    qa = jnp.concatenate([qT4, jnp.concatenate([bias] * NSA_G, axis=1)], axis=0)

    ch = s_scr.shape[1]
    last = (i * sq) // ch

    def scores(c):
        k0 = pl.multiple_of(c * ch, ch)
        return jnp.dot(ka_ref[0, 0, pl.ds(k0, ch), :], qa, preferred_element_type=F32)

    def pass1(c, m):
        s = scores(c)
        s_scr[c] = s
        return jnp.maximum(m, jnp.max(s, axis=0, keepdims=True))

    m = lax.fori_loop(0, last, pass1, jnp.full((1, nq4), -jnp.inf, F32))
    kpos = last * ch + lax.broadcasted_iota(jnp.int32, (ch, 1), 0)
    s = jnp.where(kpos <= t4, scores(last), NEG)
    s_scr[last] = s
    m = jnp.maximum(m, jnp.max(s, axis=0, keepdims=True))

    def pass2(c, carry):
        l, acc = carry
        k0 = pl.multiple_of(c * ch, ch)
        p = jnp.exp(s_scr[c] - m)
        l = l + jnp.sum(p, axis=0, keepdims=True)
        acc = acc + jnp.dot(vsT_ref[0, 0, :, pl.ds(k0, ch)], p.astype(BF16), preferred_element_type=F32)
        return l, acc

    l, acc = lax.fori_loop(0, last + 1, pass2, (jnp.zeros((1, nq4), F32), jnp.zeros((DH, nq4), F32)))
    o_slc = acc * (1.0 / l)

    nk = WINDOW + sq
    ks = pl.multiple_of(jnp.maximum(q0 - WINDOW, 0), sq)
    s = jnp.dot(kw_ref[0, 0, pl.ds(ks, nk), :], qw, preferred_element_type=F32)
    d = t4 - (ks + lax.broadcasted_iota(jnp.int32, (nk, 1), 0))
    s = jnp.where(jnp.logical_and(d >= 0, d < WINDOW), s, NEG)
    _, p, l = _softmax_cols(s)
    o_win = jnp.dot(vwT_ref[0, 0, :, pl.ds(ks, nk)], p.astype(BF16), preferred_element_type=F32) * (1.0 / l)

    def gate(r):
        return jnp.concatenate([g_ref[0, 0, r * NSA_G + g:r * NSA_G + g + 1, :] for g in range(NSA_G)], axis=1)

    o = gate(0) * o_cmp + gate(1) * o_slc + gate(2) * o_win
    o = jnp.concatenate([o[:, g * sq:(g + 1) * sq] for g in range(NSA_G)], axis=0)
    o_ref[0, 0] = o.T.astype(BF16)


def _attention(qT, kaug, vsT, kwin, vwT, kc, vcT, gT):
    B, _, _, S = qT.shape
    sq = ATT_SQ
    nseg = S // CMP_STRIDE
    n_slc = S // SLC_BLOCK
    ov = jnp.asarray(_overlap_matrix(S), BF16)
    per_bh = lambda *shape: pl.BlockSpec((1, 1) + shape, lambda b, h, i: (b, h, 0, 0))
    return pl.pallas_call(
        _att_kernel,
        grid=(B, NSA_KV, S // sq),
        in_specs=[
            pl.BlockSpec((1, NSA_G, DH, sq), lambda b, h, i: (b, h, 0, i)),
            per_bh(S, LANES), per_bh(DH, S), per_bh(S, LANES), per_bh(DH, S),
            per_bh(nseg, LANES), per_bh(DH, nseg),
            pl.BlockSpec((1, 1, 16, sq), lambda b, h, i: (b, h, 0, i)),
            pl.BlockSpec((n_slc, nseg), lambda b, h, i: (0, 0)),
        ],
        out_specs=pl.BlockSpec((1, 1, sq, NSA_G * DH), lambda b, h, i: (b, h, i, 0)),
        out_shape=jax.ShapeDtypeStruct((B, NSA_KV, S, NSA_G * DH), BF16),
        scratch_shapes=[pltpu.VMEM((S // ATT_CH, ATT_CH, NSA_G * sq), F32)],
        compiler_params=_cparams(("arbitrary", "arbitrary", "arbitrary")),
        name="nsa_attention",
    )(qT, kaug, vsT, kwin, vwT, kc, vcT, gT, ov)


def _ret_kernel(q_ref, kT_ref, v_ref, g_ref, din_ref, qd_ref, kd_ref, cd_ref, gn_ref, o_ref, state_ref):
    @pl.when(pl.program_id(2) == 0)
    def _():
        state_ref[...] = jnp.zeros_like(state_ref)

    q = q_ref[0]
    kT = kT_ref[0, 0]
    v = v_ref[0]
    state = state_ref[...]
    sc = jnp.dot(q, kT, preferred_element_type=F32) * din_ref[0]
    o = jnp.dot(sc.astype(BF16), v, preferred_element_type=F32)
    o = o + jnp.dot(q, state.astype(BF16), preferred_element_type=F32) * qd_ref[0]
    kd = (kT.astype(F32) * kd_ref[0]).astype(BF16)
    state_ref[...] = state * cd_ref[0] + jnp.dot(kd, v, preferred_element_type=F32)

    mean = jnp.mean(o, axis=-1, keepdims=True)
    var = jnp.mean(jnp.square(o - mean), axis=-1, keepdims=True)
    y = (o - mean) * lax.rsqrt(var + EPS) * gn_ref[0]
    o_ref[0] = (y * jax.nn.silu(g_ref[0].astype(F32))).astype(BF16)


def _retention(qr, krT, vr, gr, gn_w):
    B, S, _ = qr.shape
    C = RET_C
    H = RET_HEADS
    log_gamma = jnp.log1p(-jnp.exp2(-5.0 - jnp.arange(H, dtype=F32)))
    idx = jnp.arange(C, dtype=F32)
    diff = idx[:, None] - idx[None, :]
    din = jnp.where(diff >= 0, jnp.exp(log_gamma[:, None, None] * jnp.maximum(diff, 0.0)), 0.0)
    qd = jnp.exp(log_gamma[:, None] * (idx[None, :] + 1.0))
    qd = jnp.broadcast_to(qd[:, :, None], (H, C, RET_DV))
    kd = jnp.exp(log_gamma[:, None] * (C - 1.0 - idx[None, :]))[:, None, :]
    cd = jnp.broadcast_to(jnp.exp(log_gamma * C)[:, None, None], (H, 1, RET_DV))
    per_h = lambda *shape: pl.BlockSpec((1,) + shape, lambda b, h, n: (h, 0, 0))
    return pl.pallas_call(
        _ret_kernel,
        grid=(B, H, S // C),
        in_specs=[
            pl.BlockSpec((1, C, RET_DK), lambda b, h, n: (b, n, h)),
            pl.BlockSpec((1, 1, RET_DK, C), lambda b, h, n: (b, h, 0, n)),
            pl.BlockSpec((1, C, RET_DV), lambda b, h, n: (b, n, h)),
            pl.BlockSpec((1, C, RET_DV), lambda b, h, n: (b, n, h)),
            per_h(C, C), per_h(C, RET_DV), per_h(1, C), per_h(1, RET_DV), per_h(1, RET_DV),
        ],
        out_specs=pl.BlockSpec((1, C, RET_DV), lambda b, h, n: (b, n, h)),
        out_shape=jax.ShapeDtypeStruct((B, S, H * RET_DV), BF16),
        scratch_shapes=[pltpu.VMEM((RET_DK, RET_DV), F32)],
        compiler_params=_cparams(("arbitrary", "arbitrary", "arbitrary")),
        name="retention",
    )(qr, krT, vr, gr, din, qd, kd, cd, gn_w.reshape(H, 1, RET_DV))


def _mix_kernel(x_ref, nw_ref, oa_ref, or_ref, wg_ref, wn_ref, wr_ref, wm_ref, o_ref):
    d = x_ref.shape[2]
    x = x_ref[0]
    hb = _rms(x, nw_ref[...]).astype(BF16)
    gate = jax.nn.sigmoid(jnp.dot(hb, wg_ref[...], preferred_element_type=F32))
    kw = NSA_G * DH
    y_a = jnp.dot(oa_ref[0, 0], wn_ref[0:kw, :], preferred_element_type=F32)
    for h in range(1, NSA_KV):
        y_a = y_a + jnp.dot(oa_ref[0, h], wn_ref[h * kw:(h + 1) * kw, :], preferred_element_type=F32)
    y_b = jnp.dot(or_ref[0], wr_ref[...], preferred_element_type=F32)
    mix = (gate[:, :d] * y_a + gate[:, d:] * y_b).astype(BF16)
    o_ref[0] = x + jnp.dot(mix, wm_ref[...], preferred_element_type=F32)


def _mix(x, norm_w, oa, o_r, wg, wn, wr, wm):
    B, S, D = x.shape
    tm = MIX_TM
    row = lambda b, i: (b, i, 0)
    full = lambda a: pl.BlockSpec(a.shape, lambda b, i: (0,) * a.ndim)
    nw = norm_w.reshape(1, D)
    return pl.pallas_call(
        _mix_kernel,
        grid=(B, S // tm),
        in_specs=[
            pl.BlockSpec((1, tm, D), row), full(nw),
            pl.BlockSpec((1, NSA_KV, tm, NSA_G * DH), lambda b, i: (b, 0, i, 0)),
            pl.BlockSpec((1, tm, RET_HEADS * RET_DV), row),
            full(wg), full(wn), full(wr), full(wm),
        ],
        out_specs=pl.BlockSpec((1, tm, D), row),
        out_shape=jax.ShapeDtypeStruct((B, S, D), F32),
        compiler_params=_cparams(("arbitrary", "arbitrary")),
        name="mixer_merge",
    )(x, nw, oa, o_r, wg, wn, wr, wm)


def _ffn_kernel(x_ref, halo_ref, nw_ref, wu_ref, cw_ref, cb_ref, wd_ref, fw_ref, o_ref, acc_ref, *, final_norm):
    tm = x_ref.shape[1]
    tf = wd_ref.shape[1]
    x = x_ref[0]
    h_main = _rms(x, nw_ref[...])
    h_halo = jnp.where(pl.program_id(1) > 0, _rms(halo_ref[0], nw_ref[...]), 0.0)
    hb = jnp.concatenate([h_halo, h_main], axis=0).astype(BF16)
    acc_ref[...] = jnp.zeros_like(acc_ref)

    def step(j, carry):
        u = jnp.dot(hb, wu_ref[j], preferred_element_type=F32)
        cw = cw_ref[j]
        y = cb_ref[j]
        for k in range(CONV_WIDTH):
            lo = HALO - (CONV_WIDTH - 1) + k
            y = y + cw[k:k + 1, :] * u[lo:lo + tm, :]
        act = (jax.nn.silu(y[:, :tf]) * y[:, tf:]).astype(BF16)
        acc_ref[...] += jnp.dot(act, wd_ref[j], preferred_element_type=F32)
        return carry

    lax.fori_loop(0, wu_ref.shape[0], step, 0)
    y = x + acc_ref[...]
    o_ref[0] = _rms(y, fw_ref[...]) if final_norm else y


def _ffn(x, norm_w, w_up, conv_w, conv_b, w_down, final_w, final_norm):
    B, S, D = x.shape
    tm, tf = FFN_TM, FFN_TF
    dff = w_down.shape[0]
    nt = dff // tf
    pair = lambda a: jnp.concatenate([a[..., :dff].reshape(a.shape[:-1] + (nt, tf)),
                                      a[..., dff:].reshape(a.shape[:-1] + (nt, tf))], axis=-1)
    wu = jnp.swapaxes(pair(w_up.astype(BF16)), 0, 1)
    cw = jnp.swapaxes(pair(conv_w), 0, 1)
    cb = pair(conv_b)[:, None, :]
    wd = w_down.astype(BF16).reshape(nt, tf, D)
    row = lambda b, i: (b, i, 0)
    full = lambda a: pl.BlockSpec(a.shape, lambda b, i: (0,) * a.ndim)
    nw = norm_w.reshape(1, D)
    fw = final_w.reshape(1, D)
    return pl.pallas_call(
        functools.partial(_ffn_kernel, final_norm=final_norm),
        grid=(B, S // tm),
        in_specs=[
            pl.BlockSpec((1, tm, D), row),
            pl.BlockSpec((1, HALO, D), lambda b, i: (b, jnp.maximum(i * (tm // HALO) - 1, 0), 0)),
            full(nw), full(wu), full(cw), full(cb), full(wd), full(fw),
        ],
        out_specs=pl.BlockSpec((1, tm, D), row),
        out_shape=jax.ShapeDtypeStruct((B, S, D), F32),
        scratch_shapes=[pltpu.VMEM((tm, D), F32)],
        compiler_params=_cparams(("arbitrary", "arbitrary")),
        name="conv_ffn",
    )(x, x, nw, wu, cw, cb, wd, fw)


def kernel(x, norm_mix_w, w_in, cmp_pe_k, cmp_w1_k, cmp_w2_k, cmp_pe_v, cmp_w1_v, cmp_w2_v, w_nsa_branch, ret_gn_w, w_ret_branch, w_mix_out, norm_ffn_w, w_ffn_up, ffn_conv_w, ffn_conv_b, w_ffn_down, norm_final_w):
    depth = w_in.shape[0]
    d = x.shape[-1]
    for layer in range(depth):
        qT, kvc, kaug, kwin, vsT, vwT, gT, qr, krT, vr, gr = _projection(x, norm_mix_w[layer], w_in[layer])
        kc, vcT = _compress(kvc, cmp_pe_k[layer], cmp_w1_k[layer], cmp_w2_k[layer],
                            cmp_pe_v[layer], cmp_w1_v[layer], cmp_w2_v[layer])
        oa = _attention(qT, kaug, vsT, kwin, vwT, kc, vcT, gT)
        o_r = _retention(qr, krT, vr, gr, ret_gn_w[layer])
        wg = w_in[layer][:, -2 * d:].astype(BF16)
        x1 = _mix(x, norm_mix_w[layer], oa, o_r, wg, w_nsa_branch[layer].astype(BF16),
                  w_ret_branch[layer].astype(BF16), w_mix_out[layer].astype(BF16))
        x = _ffn(x1, norm_ffn_w[layer], w_ffn_up[layer], ffn_conv_w[layer], ffn_conv_b[layer],
                 w_ffn_down[layer], norm_final_w, final_norm=layer == depth - 1)
    return x
```

```python
import functools

import numpy as np
import jax
import jax.numpy as jnp
from jax import lax
from jax.experimental import pallas as pl
from jax.experimental.pallas import tpu as pltpu

F32 = jnp.float32
BF16 = jnp.bfloat16

NSA_HEADS = 8
NSA_KV = 2
NSA_G = NSA_HEADS // NSA_KV
DH = 64
ROPE_DIM = DH // 4
ROPE_THETA = 500000.0
CMP_BLOCK = 32
CMP_STRIDE = 16
CMP_HIDDEN = 256
SLC_BLOCK = 64
SLC_TOPK = 16
WINDOW = 512
RET_HEADS = 4
RET_DK = 128
RET_DV = 256
RET_THETA = 10000.0
CONV_WIDTH = 3
EPS = 1e-6
NEG = -1e30
LOG2E = 1.4426950408889634

LANES = 128
VMEM_LIMIT = 56 * 1024 * 1024

PROJ_TM = 256
ATT_SQ = 256
ATT_CH = 512
ONES_ROWS = 16
RET_C = 256
MIX_TM = 512
FFN_TM = 512
FFN_TF = 256
HALO = 8


def _cparams(sem):
    return pltpu.CompilerParams(dimension_semantics=sem, vmem_limit_bytes=VMEM_LIMIT)


def _rms(x, w):
    return x * lax.rsqrt(jnp.mean(x * x, axis=-1, keepdims=True) + EPS) * w


_C_Q = (0, 512)
_C_KVC = (512, 768)
_C_KSLC = (768, 1024)
_C_KWIN = (1024, 1280)
_C_VSW = (1280, 1536)
_C_G = (1536, 1664)
_C_QR = (1664, 2176)
_C_KR = (2176, 2688)
_C_VR = (2688, 3712)
_C_GR = (3712, 4736)
_PROJ_COLS = 4736


def _proj_kernel(x_ref, nw_ref, w_ref, cn_ref, s1_ref, s2_ref, cr_ref, sr_ref,
                 qT_ref, kvc_ref, kslc_ref, kwin_ref, vsT_ref, vwT_ref, gT_ref,
                 qr_ref, krT_ref, vr_ref, gr_ref, *, seq):
    tm = x_ref.shape[1]
    i = pl.program_id(1)
    hb = _rms(x_ref[0], nw_ref[...]).astype(BF16)

    def mm(c):
        return jnp.dot(hb, w_ref[:, c[0]:c[1]], preferred_element_type=F32)

    cn, s1, s2 = cn_ref[...], s1_ref[...], s2_ref[...]

    def rope_nsa(t):
        return t * cn + pltpu.roll(t, 8, 1) * s1 + pltpu.roll(t, LANES - 8, 1) * s2

    acc = mm(_C_Q)
    for j in range(4):
        t = rope_nsa(acc[:, j * LANES:(j + 1) * LANES]) * (DH ** -0.5 * LOG2E)
        qT_ref[0, 2 * j:2 * j + 2] = t.T.astype(BF16).reshape(2, DH, tm)

    acc = mm(_C_KVC)
    for kind in range(2):
        kvc_ref[0, kind] = acc[:, kind * LANES:(kind + 1) * LANES]

    lane = lax.broadcasted_iota(jnp.int32, (tm, LANES), 1)
    pos = i * tm + lax.broadcasted_iota(jnp.int32, (tm, LANES), 0)
    onehot = jnp.where(lane - DH == pos // SLC_BLOCK, 1.0, 0.0)
    acc = mm(_C_KSLC)
    for h in range(NSA_KV):
        t = rope_nsa(acc[:, h * LANES:(h + 1) * LANES])
        kslc_ref[0, h] = jnp.where(lane >= DH, onehot, t).astype(BF16)
    acc = mm(_C_KWIN)
    for h in range(NSA_KV):
        kwin_ref[0, h] = rope_nsa(acc[:, h * LANES:(h + 1) * LANES]).astype(BF16)

    acc = mm(_C_VSW)
    vsT_ref[0] = acc[:, :LANES].T.astype(BF16).reshape(NSA_KV, DH, tm)
    vwT_ref[0] = acc[:, LANES:].T.astype(BF16).reshape(NSA_KV, DH, tm)

    g = jax.nn.sigmoid(mm(_C_G)).T
    gT_ref[0] = g[:32].reshape(NSA_KV, 16, tm)

    cr, sr = cr_ref[...], sr_ref[...]

    def rope_ret(t):
        return t * cr + pltpu.roll(t, RET_DK // 2, 1) * sr

    acc = mm(_C_QR)
    for j in range(RET_HEADS):
        qr_ref[0, :, j * LANES:(j + 1) * LANES] = rope_ret(acc[:, j * LANES:(j + 1) * LANES]).astype(BF16)
    acc = mm(_C_KR)
    for j in range(RET_HEADS):
        t = rope_ret(acc[:, j * LANES:(j + 1) * LANES]) * (RET_DK ** -0.5)
        krT_ref[0, j] = t.T.astype(BF16)
    vr_ref[0] = mm(_C_VR).astype(BF16)
    gr_ref[0] = mm(_C_GR).astype(BF16)


def _pack_proj_weight(w_in):
    w = w_in.astype(BF16)
    d = w.shape[0]
    z64 = jnp.zeros((d, DH), BF16)
    kv0 = NSA_HEADS * DH

    def kv(r, h):
        c = kv0 + r * NSA_KV * DH + h * DH
        return w[:, c:c + DH]

    g0 = kv0 + 6 * NSA_KV * DH
    gidx = []
    for h in range(NSA_KV):
        for r in range(3):
            for g in range(NSA_G):
                gidx.append(g0 + (h * NSA_G + g) * 3 + r)
        gidx += [g0] * 4
    gcols = w[:, np.asarray(gidx)]
    r0 = g0 + 3 * NSA_HEADS
    parts = [
        w[:, :kv0],
        w[:, kv0:kv0 + 2 * NSA_KV * DH],
        kv(2, 0), z64, kv(2, 1), z64,
        kv(4, 0), z64, kv(4, 1), z64,
        kv(3, 0), kv(3, 1), kv(5, 0), kv(5, 1),
        gcols, jnp.zeros((d, LANES - 32), BF16),
        w[:, r0:r0 + 2 * RET_HEADS * RET_DK + 2 * RET_HEADS * RET_DV],
    ]
    out = jnp.concatenate(parts, axis=1)
    assert out.shape[1] == _PROJ_COLS
    return out


def _rope_tables(seq):
    pos = jnp.arange(seq, dtype=F32)[:, None]
    half = ROPE_DIM // 2
    inv = jnp.power(jnp.float32(ROPE_THETA), -jnp.arange(half, dtype=F32) / half)
    ang = pos * inv[None, :]
    cos, sin = jnp.cos(ang), jnp.sin(ang)
    ones = jnp.ones((seq, DH - ROPE_DIM), F32)
    zeros = jnp.zeros((seq, DH - ROPE_DIM), F32)
    zh = jnp.zeros((seq, half), F32)
    cn = jnp.concatenate([cos, cos, ones], axis=1)
    s1 = jnp.concatenate([zh, sin, zeros], axis=1)
    s2 = jnp.concatenate([-sin, zh, zeros], axis=1)
    tile2 = lambda t: jnp.concatenate([t, t], axis=1)
    half_r = RET_DK // 2
    inv_r = jnp.power(jnp.float32(RET_THETA), -jnp.arange(half_r, dtype=F32) / half_r)
    ang_r = pos * inv_r[None, :]
    cos_r, sin_r = jnp.cos(ang_r), jnp.sin(ang_r)
    cr = jnp.concatenate([cos_r, cos_r], axis=1)
    sr = jnp.concatenate([-sin_r, sin_r], axis=1)
    return tile2(cn), tile2(s1), tile2(s2), cr, sr


def _projection(x, norm_w, w_in):
    B, S, D = x.shape
    tm = PROJ_TM
    wp = _pack_proj_weight(w_in)
    tabs = _rope_tables(S)
    row = lambda b, i: (b, i, 0)
    tab_spec = pl.BlockSpec((tm, LANES), lambda b, i: (i, 0))
    out_shape = (
        jax.ShapeDtypeStruct((B, NSA_HEADS, DH, S), BF16),
        jax.ShapeDtypeStruct((B, 2, S, NSA_KV * DH), F32),
        jax.ShapeDtypeStruct((B, NSA_KV, S, LANES), BF16),
        jax.ShapeDtypeStruct((B, NSA_KV, S, LANES), BF16),
        jax.ShapeDtypeStruct((B, NSA_KV, DH, S), BF16),
        jax.ShapeDtypeStruct((B, NSA_KV, DH, S), BF16),
        jax.ShapeDtypeStruct((B, NSA_KV, 16, S), F32),
        jax.ShapeDtypeStruct((B, S, RET_HEADS * RET_DK), BF16),
        jax.ShapeDtypeStruct((B, RET_HEADS, RET_DK, S), BF16),
        jax.ShapeDtypeStruct((B, S, RET_HEADS * RET_DV), BF16),
        jax.ShapeDtypeStruct((B, S, RET_HEADS * RET_DV), BF16),
    )
    tok_last = lambda n, d: pl.BlockSpec((1, n, d, tm), lambda b, i: (b, 0, 0, i))
    out_specs = (
        tok_last(NSA_HEADS, DH),
        pl.BlockSpec((1, 2, tm, NSA_KV * DH), lambda b, i: (b, 0, i, 0)),
        pl.BlockSpec((1, NSA_KV, tm, LANES), lambda b, i: (b, 0, i, 0)),
        pl.BlockSpec((1, NSA_KV, tm, LANES), lambda b, i: (b, 0, i, 0)),
        tok_last(NSA_KV, DH),
        tok_last(NSA_KV, DH),
        tok_last(NSA_KV, 16),
        pl.BlockSpec((1, tm, RET_HEADS * RET_DK), row),
        tok_last(RET_HEADS, RET_DK),
        pl.BlockSpec((1, tm, RET_HEADS * RET_DV), row),
        pl.BlockSpec((1, tm, RET_HEADS * RET_DV), row),
    )
    return pl.pallas_call(
        functools.partial(_proj_kernel, seq=S),
        grid=(B, S // tm),
        in_specs=[
            pl.BlockSpec((1, tm, D), row),
            pl.BlockSpec((1, D), lambda b, i: (0, 0)),
            pl.BlockSpec((D, _PROJ_COLS), lambda b, i: (0, 0)),
            tab_spec, tab_spec, tab_spec, tab_spec, tab_spec,
        ],
        out_specs=out_specs,
        out_shape=out_shape,
        compiler_params=_cparams(("arbitrary", "arbitrary")),
        name="in_proj",
    )(x, norm_w.reshape(1, D), wp, *tabs)


def _cmp_kernel(kvc_ref, pek_ref, pev_ref, w1k_ref, w1v_ref, w2k_ref, w2v_ref, kc_ref, vcT_ref):
    nseg = kvc_ref.shape[2] // CMP_STRIDE
    half = CMP_BLOCK // 2
    row = lax.broadcasted_iota(jnp.int32, (nseg, 1), 0)
    for kind, (pe_ref, w1_ref, w2_ref) in enumerate(((pek_ref, w1k_ref, w2k_ref), (pev_ref, w1v_ref, w2v_ref))):
        acc_a = [jnp.zeros((nseg, CMP_HIDDEN), F32) for _ in range(NSA_KV)]
        acc_b = [jnp.zeros((nseg, CMP_HIDDEN), F32) for _ in range(NSA_KV)]
        for l in range(half):
            xl = kvc_ref[0, kind, pl.ds(l, nseg, stride=CMP_STRIDE), :]
            xa = (xl + pe_ref[l:l + 1, :]).astype(BF16)
            xb = (xl + pe_ref[half + l:half + l + 1, :]).astype(BF16)
            for h in range(NSA_KV):
                acc_a[h] += jnp.dot(xa[:, h * DH:(h + 1) * DH], w1_ref[l], preferred_element_type=F32)
                acc_b[h] += jnp.dot(xb[:, h * DH:(h + 1) * DH], w1_ref[half + l], preferred_element_type=F32)
        for h in range(NSA_KV):
            nxt = jnp.concatenate([acc_b[h][1:], jnp.zeros((1, CMP_HIDDEN), F32)], axis=0)
            hid = jax.nn.gelu(acc_a[h] + nxt).astype(BF16)
            out = jnp.dot(hid, w2_ref[...], preferred_element_type=F32)
            out = jnp.where(row < nseg - 1, out, 0.0)
            if kind == 0:
                kc_ref[0, h] = jnp.concatenate([out, jnp.zeros((nseg, LANES - DH), F32)], axis=1).astype(BF16)
            else:
                vcT_ref[0, h] = out.T.astype(BF16)


def _compress(kvc, pe_k, w1_k, w2_k, pe_v, w1_v, w2_v):
    B, _, S, _ = kvc.shape
    nseg = S // CMP_STRIDE
    tile2 = lambda t: jnp.concatenate([t, t], axis=1)
    w1 = lambda w: w.astype(BF16).reshape(CMP_BLOCK, DH, CMP_HIDDEN)
    full = lambda *shape: pl.BlockSpec(shape, lambda b: (0,) * len(shape))
    return pl.pallas_call(
        _cmp_kernel,
        grid=(B,),
        in_specs=[
            pl.BlockSpec((1, 2, S, NSA_KV * DH), lambda b: (b, 0, 0, 0)),
            full(CMP_BLOCK, LANES), full(CMP_BLOCK, LANES),
            full(CMP_BLOCK, DH, CMP_HIDDEN), full(CMP_BLOCK, DH, CMP_HIDDEN),
            full(CMP_HIDDEN, DH), full(CMP_HIDDEN, DH),
        ],
        out_specs=(
            pl.BlockSpec((1, NSA_KV, nseg, LANES), lambda b: (b, 0, 0, 0)),
            pl.BlockSpec((1, NSA_KV, DH, nseg), lambda b: (b, 0, 0, 0)),
        ),
        out_shape=(
            jax.ShapeDtypeStruct((B, NSA_KV, nseg, LANES), BF16),
            jax.ShapeDtypeStruct((B, NSA_KV, DH, nseg), BF16),
        ),
        compiler_params=_cparams(("arbitrary",)),
        name="nsa_compress",
    )(kvc, tile2(pe_k), tile2(pe_v), w1(w1_k), w1(w1_v), w2_k.astype(BF16), w2_v.astype(BF16))


def _overlap_matrix(seq):
    nseg = seq // CMP_STRIDE
    n_slc = seq // SLC_BLOCK
    cs = np.arange(nseg) * CMP_STRIDE
    ss = np.arange(n_slc) * SLC_BLOCK
    ov = np.clip(np.minimum(cs[None, :] + CMP_BLOCK, ss[:, None] + SLC_BLOCK)
                 - np.maximum(cs[None, :], ss[:, None]), 0, None) / CMP_BLOCK
    ov[:, nseg - 1] = 0.0
    return ov


def _with_ones(vT):
    return jnp.concatenate([vT, jnp.ones((ONES_ROWS, vT.shape[1]), BF16)], axis=0)


def _att_kernel(qT_ref, ka_ref, vsT_ref, kw_ref, vwT_ref, kc_ref, vcT_ref, g_ref, ov_ref, o_ref,
                s_a, s_b, cm_a, cm_b, m_scr, acc_scr):
    sq = qT_ref.shape[3]
    nq4 = NSA_G * sq
    n_slc = ov_ref.shape[0]
    nseg = kc_ref.shape[2]
    i = pl.program_id(2)
    q0 = pl.multiple_of(i * sq, sq)

    qT4 = jnp.concatenate([qT_ref[0, g] for g in range(NSA_G)], axis=1)
    qw = jnp.concatenate([qT4, jnp.zeros((LANES - DH, nq4), BF16)], axis=0)
    tq = q0 + lax.broadcasted_iota(jnp.int32, (1, sq), 1)
    t4 = jnp.concatenate([tq] * NSA_G, axis=1)

    nk = WINDOW + sq
    ks = pl.multiple_of(jnp.maximum(q0 - WINDOW, 0), sq)
    s = jnp.dot(kw_ref[0, 0, pl.ds(ks, nk), :], qw, preferred_element_type=F32)
    d = (t4 - ks) - lax.broadcasted_iota(jnp.int32, (nk, 1), 0)
    in_window = lax.bitcast_convert_type(d, jnp.uint32) < WINDOW
    s = jnp.where(in_window, s, NEG)
    p = jnp.exp2(s - jnp.max(s, axis=0, keepdims=True)).astype(BF16)
    r = jnp.dot(_with_ones(vwT_ref[0, 0, :, pl.ds(ks, nk)]), p, preferred_element_type=F32)
    o_win = r[:DH] * (1.0 / r[DH:DH + 1])

    s = jnp.dot(kc_ref[0, 0], qw, preferred_element_type=F32)
    c = lax.broadcasted_iota(jnp.int32, (nseg, 1), 0)
    cmask = c * CMP_STRIDE + (CMP_BLOCK - 1) <= t4
    s = jnp.where(cmask, s, NEG)
    m = jnp.max(s, axis=0, keepdims=True)
    p = jnp.where(cmask, jnp.exp2(s - m), 0.0)
    l = jnp.sum(p, axis=0, keepdims=True)
    pn = p * jnp.where(l > 0.0, 1.0 / l, 0.0)
    o_cmp = jnp.dot(vcT_ref[0, 0], pn.astype(BF16), preferred_element_type=F32)

    psum = pn[:, 0:sq]
    for g in range(1, NSA_G):
        psum = psum + pn[:, g * sq:(g + 1) * sq]
    p_hi = psum.astype(BF16)
    p_lo = (psum - p_hi.astype(F32)).astype(BF16)
    imp = (jnp.dot(ov_ref[...], p_hi, preferred_element_type=F32)
           + jnp.dot(ov_ref[...], p_lo, preferred_element_type=F32))
    jb = lax.broadcasted_iota(jnp.int32, (n_slc, 1), 0)
    cur = tq // SLC_BLOCK
    forced = jnp.logical_or(jb == 0, jnp.logical_or(jb == cur, jb == cur - 1))
    imp = jnp.where(forced, jnp.inf, jnp.where(jb <= cur, imp, -jnp.inf))
    sub = lax.broadcasted_iota(jnp.int32, (8, 1), 0)
    groups = [imp[8 * v:8 * v + 8] for v in range(n_slc // 8)]
    ranks = [jnp.zeros((8, sq), F32) for _ in groups]
    for jp in range(n_slc):
        r = imp[jp:jp + 1, :]
        for v, x in enumerate(groups):
            ge = jnp.where(r >= x, 1.0, 0.0)
            gt = jnp.where(r > x, 1.0, 0.0)
            if jp < 8 * v:
                cnt = ge
            elif jp >= 8 * v + 8:
                cnt = gt
            else:
                cnt = jnp.where(sub > jp - 8 * v, ge, gt)
            ranks[v] = ranks[v] + cnt
    rank = jnp.concatenate(ranks, axis=0)
    bias = jnp.where(rank < float(min(SLC_TOPK, n_slc)), 0.0, NEG).astype(BF16)
    if n_slc < LANES - DH:
        bias = jnp.concatenate([bias, jnp.zeros((LANES - DH - n_slc, sq), BF16)], axis=0)
    qa = jnp.concatenate([qT4, jnp.concatenate([bias] * NSA_G, axis=1)], axis=0)

    ch = s_a.shape[0]
    last = (i * sq) // ch
    slots = ((s_a, cm_a), (s_b, cm_b))

    def produce(c, slot, causal=False):
        s_ref, cm_ref = slots[slot]
        k0 = pl.multiple_of(c * ch, ch)
        s = jnp.dot(ka_ref[0, 0, pl.ds(k0, ch), :], qa, preferred_element_type=F32)
        if causal:
            s = jnp.where(k0 + lax.broadcasted_iota(jnp.int32, (ch, 1), 0) <= t4, s, NEG)
        s_ref[...] = s
        cm_ref[...] = jnp.max(s, axis=0, keepdims=True)

    def consume(c, slot):
        s_ref, cm_ref = slots[slot]
        k0 = pl.multiple_of(c * ch, ch)
        m_old = m_scr[...]
        m_new = jnp.maximum(m_old, cm_ref[...])
        p = jnp.exp2(s_ref[...] - m_new).astype(BF16)
        pv = jnp.dot(_with_ones(vsT_ref[0, 0, :, pl.ds(k0, ch)]), p, preferred_element_type=F32)
        acc_scr[...] = jnp.exp2(m_old - m_new) * acc_scr[...] + pv
        m_scr[...] = m_new

    m_scr[...] = jnp.full_like(m_scr, -jnp.inf)
    acc_scr[...] = jnp.zeros_like(acc_scr)
    produce(last, 0, causal=True)

    def pair(j, carry):
        produce(2 * j, 1)
        consume(jnp.where(j == 0, last, 2 * j - 1), 0)
        produce(2 * j + 1, 0)
        consume(2 * j, 1)
        return carry

    npairs = last // 2
    lax.fori_loop(0, npairs, pair, 0)
    held = jnp.where(npairs == 0, last, 2 * npairs - 1)

    @pl.when(last % 2 == 1)
    def _():
        produce(last - 1, 1)
        consume(held, 0)
        consume(last - 1, 1)

    @pl.when(last % 2 == 0)
    def _():
        consume(held, 0)

    o_slc = acc_scr[:DH] * (1.0 / acc_scr[DH:DH + 1])

    def gate(r):
        return jnp.concatenate([g_ref[0, 0, r * NSA_G + g:r * NSA_G + g + 1, :] for g in range(NSA_G)], axis=1)

    o = gate(0) * o_cmp + gate(1) * o_slc + gate(2) * o_win
    o = jnp.concatenate([o[:, g * sq:(g + 1) * sq] for g in range(NSA_G)], axis=0)
    o_ref[0, 0] = o.T.astype(BF16)


def _attention(qT, kaug, vsT, kwin, vwT, kc, vcT, gT):
    B, _, _, S = qT.shape
    sq = ATT_SQ
    nseg = S // CMP_STRIDE
    n_slc = S // SLC_BLOCK
    ov = jnp.asarray(_overlap_matrix(S), BF16)
    per_bh = lambda *shape: pl.BlockSpec((1, 1) + shape, lambda b, h, i: (b, h, 0, 0))
    return pl.pallas_call(
        _att_kernel,
        grid=(B, NSA_KV, S // sq),
        in_specs=[
            pl.BlockSpec((1, NSA_G, DH, sq), lambda b, h, i: (b, h, 0, i)),
            per_bh(S, LANES), per_bh(DH, S), per_bh(S, LANES), per_bh(DH, S),
            per_bh(nseg, LANES), per_bh(DH, nseg),
            pl.BlockSpec((1, 1, 16, sq), lambda b, h, i: (b, h, 0, i)),
            pl.BlockSpec((n_slc, nseg), lambda b, h, i: (0, 0)),
        ],
        out_specs=pl.BlockSpec((1, 1, sq, NSA_G * DH), lambda b, h, i: (b, h, i, 0)),
        out_shape=jax.ShapeDtypeStruct((B, NSA_KV, S, NSA_G * DH), BF16),
        scratch_shapes=[
            pltpu.VMEM((ATT_CH, NSA_G * sq), F32),
            pltpu.VMEM((ATT_CH, NSA_G * sq), F32),
            pltpu.VMEM((1, NSA_G * sq), F32),
            pltpu.VMEM((1, NSA_G * sq), F32),
            pltpu.VMEM((1, NSA_G * sq), F32),
            pltpu.VMEM((DH + ONES_ROWS, NSA_G * sq), F32),
        ],
        compiler_params=_cparams(("arbitrary", "arbitrary", "arbitrary")),
        name="nsa_attention",
    )(qT, kaug, vsT, kwin, vwT, kc, vcT, gT, ov)


def _ret_kernel(q_ref, kT_ref, v_ref, g_ref, din_ref, qd_ref, kd_ref, cd_ref, gn_ref, o_ref, state_ref):
    @pl.when(pl.program_id(2) == 0)
    def _():
        state_ref[...] = jnp.zeros_like(state_ref)

    q = q_ref[0]
    kT = kT_ref[0, 0]
    v = v_ref[0]
    state = state_ref[...]
    sc = jnp.dot(q, kT, preferred_element_type=F32) * din_ref[0]
    o = jnp.dot(sc.astype(BF16), v, preferred_element_type=F32)
    o = o + jnp.dot(q, state.astype(BF16), preferred_element_type=F32) * qd_ref[0]
    kd = (kT.astype(F32) * kd_ref[0]).astype(BF16)
    state_ref[...] = state * cd_ref[0] + jnp.dot(kd, v, preferred_element_type=F32)

    mean = jnp.mean(o, axis=-1, keepdims=True)
    var = jnp.mean(jnp.square(o - mean), axis=-1, keepdims=True)
    y = (o - mean) * lax.rsqrt(var + EPS) * gn_ref[0]
    o_ref[0] = (y * jax.nn.silu(g_ref[0].astype(F32))).astype(BF16)


def _retention(qr, krT, vr, gr, gn_w):
    B, S, _ = qr.shape
    C = RET_C
    H = RET_HEADS
    log_gamma = jnp.log1p(-jnp.exp2(-5.0 - jnp.arange(H, dtype=F32)))
    idx = jnp.arange(C, dtype=F32)
    diff = idx[:, None] - idx[None, :]
    din = jnp.where(diff >= 0, jnp.exp(log_gamma[:, None, None] * jnp.maximum(diff, 0.0)), 0.0)
    qd = jnp.exp(log_gamma[:, None] * (idx[None, :] + 1.0))
    qd = jnp.broadcast_to(qd[:, :, None], (H, C, RET_DV))
    kd = jnp.exp(log_gamma[:, None] * (C - 1.0 - idx[None, :]))[:, None, :]
    cd = jnp.broadcast_to(jnp.exp(log_gamma * C)[:, None, None], (H, 1, RET_DV))
    per_h = lambda *shape: pl.BlockSpec((1,) + shape, lambda b, h, n: (h, 0, 0))
    return pl.pallas_call(
        _ret_kernel,
        grid=(B, H, S // C),
        in_specs=[
            pl.BlockSpec((1, C, RET_DK), lambda b, h, n: (b, n, h)),
            pl.BlockSpec((1, 1, RET_DK, C), lambda b, h, n: (b, h, 0, n)),
            pl.BlockSpec((1, C, RET_DV), lambda b, h, n: (b, n, h)),
            pl.BlockSpec((1, C, RET_DV), lambda b, h, n: (b, n, h)),
            per_h(C, C), per_h(C, RET_DV), per_h(1, C), per_h(1, RET_DV), per_h(1, RET_DV),
        ],
        out_specs=pl.BlockSpec((1, C, RET_DV), lambda b, h, n: (b, n, h)),
        out_shape=jax.ShapeDtypeStruct((B, S, H * RET_DV), BF16),
        scratch_shapes=[pltpu.VMEM((RET_DK, RET_DV), F32)],
        compiler_params=_cparams(("arbitrary", "arbitrary", "arbitrary")),
        name="retention",
    )(qr, krT, vr, gr, din, qd, kd, cd, gn_w.reshape(H, 1, RET_DV))


def _mix_kernel(x_ref, nw_ref, oa_ref, or_ref, wg_ref, wn_ref, wr_ref, wm_ref, o_ref):
    d = x_ref.shape[2]
    x = x_ref[0]
    hb = _rms(x, nw_ref[...]).astype(BF16)
    gate = jax.nn.sigmoid(jnp.dot(hb, wg_ref[...], preferred_element_type=F32))
    kw = NSA_G * DH
    y_a = jnp.dot(oa_ref[0, 0], wn_ref[0:kw, :], preferred_element_type=F32)
    for h in range(1, NSA_KV):
        y_a = y_a + jnp.dot(oa_ref[0, h], wn_ref[h * kw:(h + 1) * kw, :], preferred_element_type=F32)
    y_b = jnp.dot(or_ref[0], wr_ref[...], preferred_element_type=F32)
    mix = (gate[:, :d] * y_a + gate[:, d:] * y_b).astype(BF16)
    o_ref[0] = x + jnp.dot(mix, wm_ref[...], preferred_element_type=F32)


def _mix(x, norm_w, oa, o_r, wg, wn, wr, wm):
    B, S, D = x.shape
    tm = MIX_TM
    row = lambda b, i: (b, i, 0)
    full = lambda a: pl.BlockSpec(a.shape, lambda b, i: (0,) * a.ndim)
    nw = norm_w.reshape(1, D)
    return pl.pallas_call(
        _mix_kernel,
        grid=(B, S // tm),
        in_specs=[
            pl.BlockSpec((1, tm, D), row), full(nw),
            pl.BlockSpec((1, NSA_KV, tm, NSA_G * DH), lambda b, i: (b, 0, i, 0)),
            pl.BlockSpec((1, tm, RET_HEADS * RET_DV), row),
            full(wg), full(wn), full(wr), full(wm),
        ],
        out_specs=pl.BlockSpec((1, tm, D), row),
        out_shape=jax.ShapeDtypeStruct((B, S, D), F32),
        compiler_params=_cparams(("arbitrary", "arbitrary")),
        name="mixer_merge",
    )(x, nw, oa, o_r, wg, wn, wr, wm)


def _ffn_kernel(x_ref, halo_ref, nw_ref, wu_ref, cw_ref, cb_ref, wd_ref, fw_ref, o_ref, acc_ref, *, final_norm):
    tm = x_ref.shape[1]
    tf = wd_ref.shape[1]
    x = x_ref[0]
    h_main = _rms(x, nw_ref[...])
    h_halo = jnp.where(pl.program_id(1) > 0, _rms(halo_ref[0], nw_ref[...]), 0.0)
    hb = jnp.concatenate([h_halo, h_main], axis=0).astype(BF16)
    acc_ref[...] = jnp.zeros_like(acc_ref)

    def step(j, carry):
        u = jnp.dot(hb, wu_ref[j], preferred_element_type=F32)
        cw = cw_ref[j]
        y = cb_ref[j]
        for k in range(CONV_WIDTH):
            lo = HALO - (CONV_WIDTH - 1) + k
            y = y + cw[k:k + 1, :] * u[lo:lo + tm, :]
        act = (jax.nn.silu(y[:, :tf]) * y[:, tf:]).astype(BF16)
        acc_ref[...] += jnp.dot(act, wd_ref[j], preferred_element_type=F32)
        return carry

    lax.fori_loop(0, wu_ref.shape[0], step, 0)
    y = x + acc_ref[...]
    o_ref[0] = _rms(y, fw_ref[...]) if final_norm else y


def _ffn(x, norm_w, w_up, conv_w, conv_b, w_down, final_w, final_norm):
    B, S, D = x.shape
    tm, tf = FFN_TM, FFN_TF
    dff = w_down.shape[0]
    nt = dff // tf
    pair = lambda a: jnp.concatenate([a[..., :dff].reshape(a.shape[:-1] + (nt, tf)),
                                      a[..., dff:].reshape(a.shape[:-1] + (nt, tf))], axis=-1)
    wu = jnp.swapaxes(pair(w_up.astype(BF16)), 0, 1)
    cw = jnp.swapaxes(pair(conv_w), 0, 1)
    cb = pair(conv_b)[:, None, :]
    wd = w_down.astype(BF16).reshape(nt, tf, D)
    row = lambda b, i: (b, i, 0)
    full = lambda a: pl.BlockSpec(a.shape, lambda b, i: (0,) * a.ndim)
    nw = norm_w.reshape(1, D)
    fw = final_w.reshape(1, D)
    return pl.pallas_call(
        functools.partial(_ffn_kernel, final_norm=final_norm),
        grid=(B, S // tm),
        in_specs=[
            pl.BlockSpec((1, tm, D), row),
            pl.BlockSpec((1, HALO, D), lambda b, i: (b, jnp.maximum(i * (tm // HALO) - 1, 0), 0)),
            full(nw), full(wu), full(cw), full(cb), full(wd), full(fw),
        ],
        out_specs=pl.BlockSpec((1, tm, D), row),
        out_shape=jax.ShapeDtypeStruct((B, S, D), F32),
        scratch_shapes=[pltpu.VMEM((tm, D), F32)],
        compiler_params=_cparams(("arbitrary", "arbitrary")),
        name="conv_ffn",
    )(x, x, nw, wu, cw, cb, wd, fw)


def kernel(x, norm_mix_w, w_in, cmp_pe_k, cmp_w1_k, cmp_w2_k, cmp_pe_v, cmp_w1_v, cmp_w2_v, w_nsa_branch, ret_gn_w, w_ret_branch, w_mix_out, norm_ffn_w, w_ffn_up, ffn_conv_w, ffn_conv_b, w_ffn_down, norm_final_w):
    depth = w_in.shape[0]
    d = x.shape[-1]
    for layer in range(depth):
        qT, kvc, kaug, kwin, vsT, vwT, gT, qr, krT, vr, gr = _projection(x, norm_mix_w[layer], w_in[layer])
        kc, vcT = _compress(kvc, cmp_pe_k[layer], cmp_w1_k[layer], cmp_w2_k[layer],
                            cmp_pe_v[layer], cmp_w1_v[layer], cmp_w2_v[layer])
        oa = _attention(qT, kaug, vsT, kwin, vwT, kc, vcT, gT)
        o_r = _retention(qr, krT, vr, gr, ret_gn_w[layer])
        wg = w_in[layer][:, -2 * d:].astype(BF16)
        x1 = _mix(x, norm_mix_w[layer], oa, o_r, wg, w_nsa_branch[layer].astype(BF16),
                  w_ret_branch[layer].astype(BF16), w_mix_out[layer].astype(BF16))
        x = _ffn(x1, norm_ffn_w[layer], w_ffn_up[layer], ffn_conv_w[layer], ffn_conv_b[layer],
                 w_ffn_down[layer], norm_final_w, final_norm=layer == depth - 1)
    return x
```

```python
import functools

import numpy as np
import jax
import jax.numpy as jnp
from jax import lax
from jax.experimental import pallas as pl
from jax.experimental.pallas import tpu as pltpu

F32 = jnp.float32
BF16 = jnp.bfloat16

NSA_HEADS = 8
NSA_KV = 2
NSA_G = NSA_HEADS // NSA_KV
DH = 64
ROPE_DIM = DH // 4
ROPE_THETA = 500000.0
CMP_BLOCK = 32
CMP_STRIDE = 16
CMP_HIDDEN = 256
SLC_BLOCK = 64
SLC_TOPK = 16
WINDOW = 512
RET_HEADS = 4
RET_DK = 128
RET_DV = 256
RET_THETA = 10000.0
CONV_WIDTH = 3
EPS = 1e-6
NEG = -1e30
LOG2E = 1.4426950408889634

LANES = 128
VMEM_LIMIT = 56 * 1024 * 1024

PROJ_TM = 256
ATT_SQ = 256
ATT_CH = 512
ONES_ROWS = 16
RET_C = 256
MIX_TM = 512
FFN_TM = 512
FFN_TF = 256
HALO = 16


def _cparams(sem):
    return pltpu.CompilerParams(dimension_semantics=sem, vmem_limit_bytes=VMEM_LIMIT)


def _rms(x, w):
    return x * lax.rsqrt(jnp.mean(x * x, axis=-1, keepdims=True) + EPS) * w


_C_Q = (0, 512)
_C_KVC = (512, 768)
_C_KSLC = (768, 1024)
_C_KWIN = (1024, 1280)
_C_VSW = (1280, 1536)
_C_G = (1536, 1664)
_C_QR = (1664, 2176)
_C_KR = (2176, 2688)
_C_VR = (2688, 3712)
_C_GR = (3712, 4736)
_PROJ_COLS = 4736


def _proj_kernel(x_ref, nw_ref, w_ref, cn_ref, s1_ref, s2_ref, cr_ref, sr_ref,
                 qT_ref, kvc_ref, kslc_ref, kwin_ref, vsT_ref, vwT_ref, gT_ref,
                 qr_ref, krT_ref, vr_ref, gr_ref, *, seq):
    tm = x_ref.shape[1]
    i = pl.program_id(1)
    hb = _rms(x_ref[0], nw_ref[...]).astype(BF16)

    def mm(c):
        return jnp.dot(hb, w_ref[:, c[0]:c[1]], preferred_element_type=F32)

    cn, s1, s2 = cn_ref[...], s1_ref[...], s2_ref[...]

    def rope_nsa(t):
        return t * cn + pltpu.roll(t, 8, 1) * s1 + pltpu.roll(t, LANES - 8, 1) * s2

    acc = mm(_C_Q)
    for j in range(4):
        t = rope_nsa(acc[:, j * LANES:(j + 1) * LANES]) * (DH ** -0.5 * LOG2E)
        qT_ref[0, 2 * j:2 * j + 2] = t.T.astype(BF16).reshape(2, DH, tm)

    acc = mm(_C_KVC)
    for kind in range(2):
        kvc_ref[0, kind] = acc[:, kind * LANES:(kind + 1) * LANES]

    lane = lax.broadcasted_iota(jnp.int32, (tm, LANES), 1)
    pos = i * tm + lax.broadcasted_iota(jnp.int32, (tm, LANES), 0)
    onehot = jnp.where(lane - DH == pos // SLC_BLOCK, 1.0, 0.0)
    acc = mm(_C_KSLC)
    for h in range(NSA_KV):
        t = rope_nsa(acc[:, h * LANES:(h + 1) * LANES])
        kslc_ref[0, h] = jnp.where(lane >= DH, onehot, t).astype(BF16)
    acc = mm(_C_KWIN)
    for h in range(NSA_KV):
        kwin_ref[0, h] = rope_nsa(acc[:, h * LANES:(h + 1) * LANES]).astype(BF16)

    acc = mm(_C_VSW)
    vsT_ref[0] = acc[:, :LANES].T.astype(BF16).reshape(NSA_KV, DH, tm)
    vwT_ref[0] = acc[:, LANES:].T.astype(BF16).reshape(NSA_KV, DH, tm)

    g = jax.nn.sigmoid(mm(_C_G)).T
    gT_ref[0] = g[:32].reshape(NSA_KV, 16, tm)

    cr, sr = cr_ref[...], sr_ref[...]

    def rope_ret(t):
        return t * cr + pltpu.roll(t, RET_DK // 2, 1) * sr

    acc = mm(_C_QR)
    for j in range(RET_HEADS):
        qr_ref[0, :, j * LANES:(j + 1) * LANES] = rope_ret(acc[:, j * LANES:(j + 1) * LANES]).astype(BF16)
    acc = mm(_C_KR)
    for j in range(RET_HEADS):
        t = rope_ret(acc[:, j * LANES:(j + 1) * LANES]) * (RET_DK ** -0.5)
        krT_ref[0, j] = t.T.astype(BF16)
    vr_ref[0] = mm(_C_VR).astype(BF16)
    gr_ref[0] = mm(_C_GR).astype(BF16)


def _pack_proj_weight(w_in):
    w = w_in.astype(BF16)
    d = w.shape[0]
    z64 = jnp.zeros((d, DH), BF16)
    kv0 = NSA_HEADS * DH

    def kv(r, h):
        c = kv0 + r * NSA_KV * DH + h * DH
        return w[:, c:c + DH]

    g0 = kv0 + 6 * NSA_KV * DH
    gidx = []
    for h in range(NSA_KV):
        for r in range(3):
            for g in range(NSA_G):
                gidx.append(g0 + (h * NSA_G + g) * 3 + r)
        gidx += [g0] * 4
    gcols = w[:, np.asarray(gidx)]
    r0 = g0 + 3 * NSA_HEADS
    parts = [
        w[:, :kv0],
        w[:, kv0:kv0 + 2 * NSA_KV * DH],
        kv(2, 0), z64, kv(2, 1), z64,
        kv(4, 0), z64, kv(4, 1), z64,
        kv(3, 0), kv(3, 1), kv(5, 0), kv(5, 1),
        gcols, jnp.zeros((d, LANES - 32), BF16),
        w[:, r0:r0 + 2 * RET_HEADS * RET_DK + 2 * RET_HEADS * RET_DV],
    ]
    out = jnp.concatenate(parts, axis=1)
    assert out.shape[1] == _PROJ_COLS
    return out


def _rope_tables(seq):
    pos = jnp.arange(seq, dtype=F32)[:, None]
    half = ROPE_DIM // 2
    inv = jnp.power(jnp.float32(ROPE_THETA), -jnp.arange(half, dtype=F32) / half)
    ang = pos * inv[None, :]
    cos, sin = jnp.cos(ang), jnp.sin(ang)
    ones = jnp.ones((seq, DH - ROPE_DIM), F32)
    zeros = jnp.zeros((seq, DH - ROPE_DIM), F32)
    zh = jnp.zeros((seq, half), F32)
    cn = jnp.concatenate([cos, cos, ones], axis=1)
    s1 = jnp.concatenate([zh, sin, zeros], axis=1)
    s2 = jnp.concatenate([-sin, zh, zeros], axis=1)
    tile2 = lambda t: jnp.concatenate([t, t], axis=1)
    half_r = RET_DK // 2
    inv_r = jnp.power(jnp.float32(RET_THETA), -jnp.arange(half_r, dtype=F32) / half_r)
    ang_r = pos * inv_r[None, :]
    cos_r, sin_r = jnp.cos(ang_r), jnp.sin(ang_r)
    cr = jnp.concatenate([cos_r, cos_r], axis=1)
    sr = jnp.concatenate([-sin_r, sin_r], axis=1)
    return tile2(cn), tile2(s1), tile2(s2), cr, sr


def _projection(x, norm_w, w_in):
    B, S, D = x.shape
    tm = PROJ_TM
    wp = _pack_proj_weight(w_in)
    tabs = _rope_tables(S)
    row = lambda b, i: (b, i, 0)
    tab_spec = pl.BlockSpec((tm, LANES), lambda b, i: (i, 0))
    out_shape = (
        jax.ShapeDtypeStruct((B, NSA_HEADS, DH, S), BF16),
        jax.ShapeDtypeStruct((B, 2, S, NSA_KV * DH), F32),
        jax.ShapeDtypeStruct((B, NSA_KV, S, LANES), BF16),
        jax.ShapeDtypeStruct((B, NSA_KV, S, LANES), BF16),
        jax.ShapeDtypeStruct((B, NSA_KV, DH, S), BF16),
        jax.ShapeDtypeStruct((B, NSA_KV, DH, S), BF16),
        jax.ShapeDtypeStruct((B, NSA_KV, 16, S), F32),
        jax.ShapeDtypeStruct((B, S, RET_HEADS * RET_DK), BF16),
        jax.ShapeDtypeStruct((B, RET_HEADS, RET_DK, S), BF16),
        jax.ShapeDtypeStruct((B, S, RET_HEADS * RET_DV), BF16),
        jax.ShapeDtypeStruct((B, S, RET_HEADS * RET_DV), BF16),
    )
    tok_last = lambda n, d: pl.BlockSpec((1, n, d, tm), lambda b, i: (b, 0, 0, i))
    out_specs = (
        tok_last(NSA_HEADS, DH),
        pl.BlockSpec((1, 2, tm, NSA_KV * DH), lambda b, i: (b, 0, i, 0)),
        pl.BlockSpec((1, NSA_KV, tm, LANES), lambda b, i: (b, 0, i, 0)),
        pl.BlockSpec((1, NSA_KV, tm, LANES), lambda b, i: (b, 0, i, 0)),
        tok_last(NSA_KV, DH),
        tok_last(NSA_KV, DH),
        tok_last(NSA_KV, 16),
        pl.BlockSpec((1, tm, RET_HEADS * RET_DK), row),
        tok_last(RET_HEADS, RET_DK),
        pl.BlockSpec((1, tm, RET_HEADS * RET_DV), row),
        pl.BlockSpec((1, tm, RET_HEADS * RET_DV), row),
    )
    return pl.pallas_call(
        functools.partial(_proj_kernel, seq=S),
        grid=(B, S // tm),
        in_specs=[
            pl.BlockSpec((1, tm, D), row),
            pl.BlockSpec((1, D), lambda b, i: (0, 0)),
            pl.BlockSpec((D, _PROJ_COLS), lambda b, i: (0, 0)),
            tab_spec, tab_spec, tab_spec, tab_spec, tab_spec,
        ],
        out_specs=out_specs,
        out_shape=out_shape,
        compiler_params=_cparams(("arbitrary", "arbitrary")),
        name="in_proj",
    )(x, norm_w.reshape(1, D), wp, *tabs)


def _cmp_kernel(kvc_ref, pek_ref, pev_ref, w1k_ref, w1v_ref, w2k_ref, w2v_ref, kc_ref, vcT_ref):
    nseg = kvc_ref.shape[2] // CMP_STRIDE
    half = CMP_BLOCK // 2
    row = lax.broadcasted_iota(jnp.int32, (nseg, 1), 0)
    for kind, (pe_ref, w1_ref, w2_ref) in enumerate(((pek_ref, w1k_ref, w2k_ref), (pev_ref, w1v_ref, w2v_ref))):
        acc_a = [jnp.zeros((nseg, CMP_HIDDEN), F32) for _ in range(NSA_KV)]
        acc_b = [jnp.zeros((nseg, CMP_HIDDEN), F32) for _ in range(NSA_KV)]
        for l in range(half):
            xl = kvc_ref[0, kind, pl.ds(l, nseg, stride=CMP_STRIDE), :]
            xa = (xl + pe_ref[l:l + 1, :]).astype(BF16)
            xb = (xl + pe_ref[half + l:half + l + 1, :]).astype(BF16)
            for h in range(NSA_KV):
                acc_a[h] += jnp.dot(xa[:, h * DH:(h + 1) * DH], w1_ref[l], preferred_element_type=F32)
                acc_b[h] += jnp.dot(xb[:, h * DH:(h + 1) * DH], w1_ref[half + l], preferred_element_type=F32)
        for h in range(NSA_KV):
            nxt = jnp.concatenate([acc_b[h][1:], jnp.zeros((1, CMP_HIDDEN), F32)], axis=0)
            hid = jax.nn.gelu(acc_a[h] + nxt).astype(BF16)
            out = jnp.dot(hid, w2_ref[...], preferred_element_type=F32)
            out = jnp.where(row < nseg - 1, out, 0.0)
            if kind == 0:
                kc_ref[0, h] = jnp.concatenate([out, jnp.zeros((nseg, LANES - DH), F32)], axis=1).astype(BF16)
            else:
                vcT_ref[0, h] = out.T.astype(BF16)


def _compress(kvc, pe_k, w1_k, w2_k, pe_v, w1_v, w2_v):
    B, _, S, _ = kvc.shape
    nseg = S // CMP_STRIDE
    tile2 = lambda t: jnp.concatenate([t, t], axis=1)
    w1 = lambda w: w.astype(BF16).reshape(CMP_BLOCK, DH, CMP_HIDDEN)
    full = lambda *shape: pl.BlockSpec(shape, lambda b: (0,) * len(shape))
    return pl.pallas_call(
        _cmp_kernel,
        grid=(B,),
        in_specs=[
            pl.BlockSpec((1, 2, S, NSA_KV * DH), lambda b: (b, 0, 0, 0)),
            full(CMP_BLOCK, LANES), full(CMP_BLOCK, LANES),
            full(CMP_BLOCK, DH, CMP_HIDDEN), full(CMP_BLOCK, DH, CMP_HIDDEN),
            full(CMP_HIDDEN, DH), full(CMP_HIDDEN, DH),
        ],
        out_specs=(
            pl.BlockSpec((1, NSA_KV, nseg, LANES), lambda b: (b, 0, 0, 0)),
            pl.BlockSpec((1, NSA_KV, DH, nseg), lambda b: (b, 0, 0, 0)),
        ),
        out_shape=(
            jax.ShapeDtypeStruct((B, NSA_KV, nseg, LANES), BF16),
            jax.ShapeDtypeStruct((B, NSA_KV, DH, nseg), BF16),
        ),
        compiler_params=_cparams(("arbitrary",)),
        name="nsa_compress",
    )(kvc, tile2(pe_k), tile2(pe_v), w1(w1_k), w1(w1_v), w2_k.astype(BF16), w2_v.astype(BF16))


def _overlap_matrix(seq):
    nseg = seq // CMP_STRIDE
    n_slc = seq // SLC_BLOCK
    cs = np.arange(nseg) * CMP_STRIDE
    ss = np.arange(n_slc) * SLC_BLOCK
    ov = np.clip(np.minimum(cs[None, :] + CMP_BLOCK, ss[:, None] + SLC_BLOCK)
                 - np.maximum(cs[None, :], ss[:, None]), 0, None) / CMP_BLOCK
    ov[:, nseg - 1] = 0.0
    return ov


def _with_ones(vT):
    return jnp.concatenate([vT, jnp.ones((ONES_ROWS, vT.shape[1]), BF16)], axis=0)


def _att_kernel(qT_ref, ka_ref, vsT_ref, kw_ref, vwT_ref, kc_ref, vcT_ref, g_ref, ov_ref, o_ref,
                s_a, s_b, cm_a, cm_b, m_scr, acc_scr):
    sq = qT_ref.shape[3]
    nq4 = NSA_G * sq
    n_slc = ov_ref.shape[0]
    nseg = kc_ref.shape[2]
    i = pl.program_id(2)
    q0 = pl.multiple_of(i * sq, sq)

    qT4 = jnp.concatenate([qT_ref[0, g] for g in range(NSA_G)], axis=1)
    qw = jnp.concatenate([qT4, jnp.zeros((LANES - DH, nq4), BF16)], axis=0)
    tq = q0 + lax.broadcasted_iota(jnp.int32, (1, sq), 1)
    t4 = jnp.concatenate([tq] * NSA_G, axis=1)

    nk = WINDOW + sq
    ks = pl.multiple_of(jnp.maximum(q0 - WINDOW, 0), sq)
    s = jnp.dot(kw_ref[0, 0, pl.ds(ks, nk), :], qw, preferred_element_type=F32)
    d = (t4 - ks) - lax.broadcasted_iota(jnp.int32, (nk, 1), 0)
    in_window = lax.bitcast_convert_type(d, jnp.uint32) < WINDOW
    s = jnp.where(in_window, s, NEG)
    p = jnp.exp2(s - jnp.max(s, axis=0, keepdims=True)).astype(BF16)
    r = jnp.dot(_with_ones(vwT_ref[0, 0, :, pl.ds(ks, nk)]), p, preferred_element_type=F32)
    o_win = r[:DH] * (1.0 / r[DH:DH + 1])

    s = jnp.dot(kc_ref[0, 0], qw, preferred_element_type=F32)
    c = lax.broadcasted_iota(jnp.int32, (nseg, 1), 0)
    cmask = c * CMP_STRIDE + (CMP_BLOCK - 1) <= t4
    s = jnp.where(cmask, s, NEG)
    m = jnp.max(s, axis=0, keepdims=True)
    p = jnp.where(cmask, jnp.exp2(s - m), 0.0)
    l = jnp.sum(p, axis=0, keepdims=True)
    pn = p * jnp.where(l > 0.0, 1.0 / l, 0.0)
    o_cmp = jnp.dot(vcT_ref[0, 0], pn.astype(BF16), preferred_element_type=F32)

    psum = pn[:, 0:sq]
    for g in range(1, NSA_G):
        psum = psum + pn[:, g * sq:(g + 1) * sq]
    p_hi = psum.astype(BF16)
    p_lo = (psum - p_hi.astype(F32)).astype(BF16)
    imp = (jnp.dot(ov_ref[...], p_hi, preferred_element_type=F32)
           + jnp.dot(ov_ref[...], p_lo, preferred_element_type=F32))
    jb = lax.broadcasted_iota(jnp.int32, (n_slc, 1), 0)
    cur = tq // SLC_BLOCK
    forced = jnp.logical_or(jb == 0, jnp.logical_or(jb == cur, jb == cur - 1))
    imp = jnp.where(forced, jnp.inf, jnp.where(jb <= cur, imp, -jnp.inf))
    sub = lax.broadcasted_iota(jnp.int32, (8, 1), 0)
    groups = [imp[8 * v:8 * v + 8] for v in range(n_slc // 8)]
    ranks = [jnp.zeros((8, sq), F32) for _ in groups]
    for jp in range(n_slc):
        r = imp[jp:jp + 1, :]
        for v, x in enumerate(groups):
            ge = jnp.where(r >= x, 1.0, 0.0)
            gt = jnp.where(r > x, 1.0, 0.0)
            if jp < 8 * v:
                cnt = ge
            elif jp >= 8 * v + 8:
                cnt = gt
            else:
                cnt = jnp.where(sub > jp - 8 * v, ge, gt)
            ranks[v] = ranks[v] + cnt
    rank = jnp.concatenate(ranks, axis=0)
    bias = jnp.where(rank < float(min(SLC_TOPK, n_slc)), 0.0, NEG).astype(BF16)
    if n_slc < LANES - DH:
        bias = jnp.concatenate([bias, jnp.zeros((LANES - DH - n_slc, sq), BF16)], axis=0)
    qa = jnp.concatenate([qT4, jnp.concatenate([bias] * NSA_G, axis=1)], axis=0)

    ch = s_a.shape[0]
    last = (i * sq) // ch
    slots = ((s_a, cm_a), (s_b, cm_b))

    def produce(c, slot, causal=False):
        s_ref, cm_ref = slots[slot]
        k0 = pl.multiple_of(c * ch, ch)
        s = jnp.dot(ka_ref[0, 0, pl.ds(k0, ch), :], qa, preferred_element_type=F32)
        if causal:
            s = jnp.where(k0 + lax.broadcasted_iota(jnp.int32, (ch, 1), 0) <= t4, s, NEG)
        s_ref[...] = s
        cm_ref[...] = jnp.max(s, axis=0, keepdims=True)

    def consume(c, slot):
        s_ref, cm_ref = slots[slot]
        k0 = pl.multiple_of(c * ch, ch)
        m_old = m_scr[...]
        m_new = jnp.maximum(m_old, cm_ref[...])
        p = jnp.exp2(s_ref[...] - m_new).astype(BF16)
        pv = jnp.dot(_with_ones(vsT_ref[0, 0, :, pl.ds(k0, ch)]), p, preferred_element_type=F32)
        acc_scr[...] = jnp.exp2(m_old - m_new) * acc_scr[...] + pv
        m_scr[...] = m_new

    m_scr[...] = jnp.full_like(m_scr, -jnp.inf)
    acc_scr[...] = jnp.zeros_like(acc_scr)
    produce(last, 0, causal=True)

    def pair(j, carry):
        produce(2 * j, 1)
        consume(jnp.where(j == 0, last, 2 * j - 1), 0)
        produce(2 * j + 1, 0)
        consume(2 * j, 1)
        return carry

    npairs = last // 2
    lax.fori_loop(0, npairs, pair, 0)
    held = jnp.where(npairs == 0, last, 2 * npairs - 1)

    @pl.when(last % 2 == 1)
    def _():
        produce(last - 1, 1)
        consume(held, 0)
        consume(last - 1, 1)

    @pl.when(last % 2 == 0)
    def _():
        consume(held, 0)

    o_slc = acc_scr[:DH] * (1.0 / acc_scr[DH:DH + 1])

    def gate(r):
        return jnp.concatenate([g_ref[0, 0, r * NSA_G + g:r * NSA_G + g + 1, :] for g in range(NSA_G)], axis=1)

    o = gate(0) * o_cmp + gate(1) * o_slc + gate(2) * o_win
    o = jnp.concatenate([o[:, g * sq:(g + 1) * sq] for g in range(NSA_G)], axis=0)
    o_ref[0, 0] = o.T.astype(BF16)


def _attention(qT, kaug, vsT, kwin, vwT, kc, vcT, gT):
    B, _, _, S = qT.shape
    sq = ATT_SQ
    nseg = S // CMP_STRIDE
    n_slc = S // SLC_BLOCK
    ov = jnp.asarray(_overlap_matrix(S), BF16)
    per_bh = lambda *shape: pl.BlockSpec((1, 1) + shape, lambda b, h, i: (b, h, 0, 0))
    return pl.pallas_call(
        _att_kernel,
        grid=(B, NSA_KV, S // sq),
        in_specs=[
            pl.BlockSpec((1, NSA_G, DH, sq), lambda b, h, i: (b, h, 0, i)),
            per_bh(S, LANES), per_bh(DH, S), per_bh(S, LANES), per_bh(DH, S),
            per_bh(nseg, LANES), per_bh(DH, nseg),
            pl.BlockSpec((1, 1, 16, sq), lambda b, h, i: (b, h, 0, i)),
            pl.BlockSpec((n_slc, nseg), lambda b, h, i: (0, 0)),
        ],
        out_specs=pl.BlockSpec((1, 1, sq, NSA_G * DH), lambda b, h, i: (b, h, i, 0)),
        out_shape=jax.ShapeDtypeStruct((B, NSA_KV, S, NSA_G * DH), BF16),
        scratch_shapes=[
            pltpu.VMEM((ATT_CH, NSA_G * sq), F32),
            pltpu.VMEM((ATT_CH, NSA_G * sq), F32),
            pltpu.VMEM((1, NSA_G * sq), F32),
            pltpu.VMEM((1, NSA_G * sq), F32),
            pltpu.VMEM((1, NSA_G * sq), F32),
            pltpu.VMEM((DH + ONES_ROWS, NSA_G * sq), F32),
        ],
        compiler_params=_cparams(("arbitrary", "arbitrary", "arbitrary")),
        name="nsa_attention",
    )(qT, kaug, vsT, kwin, vwT, kc, vcT, gT, ov)


def _ret_kernel(q_ref, kT_ref, v_ref, g_ref, din_ref, qd_ref, kd_ref, cd_ref, gn_ref, o_ref, state_ref):
    @pl.when(pl.program_id(1) == 0)
    def _():
        state_ref[...] = jnp.zeros_like(state_ref)

    for h in range(RET_HEADS):
        q = q_ref[0, :, h * RET_DK:(h + 1) * RET_DK]
        kT = kT_ref[0, h]
        v = v_ref[0, :, h * RET_DV:(h + 1) * RET_DV]
        state = state_ref[h]
        sc = jnp.dot(q, kT, preferred_element_type=F32) * din_ref[h]
        o = jnp.dot(sc.astype(BF16), v, preferred_element_type=F32)
        o = o + jnp.dot(q, state.astype(BF16), preferred_element_type=F32) * qd_ref[h]
        kd = (kT.astype(F32) * kd_ref[h]).astype(BF16)
        state_ref[h] = state * cd_ref[h] + jnp.dot(kd, v, preferred_element_type=F32)

        mean = jnp.mean(o, axis=-1, keepdims=True)
        var = jnp.mean(jnp.square(o - mean), axis=-1, keepdims=True)
        y = (o - mean) * lax.rsqrt(var + EPS) * gn_ref[h]
        g = g_ref[0, :, h * RET_DV:(h + 1) * RET_DV].astype(F32)
        o_ref[0, :, h * RET_DV:(h + 1) * RET_DV] = (y * jax.nn.silu(g)).astype(BF16)


def _retention(qr, krT, vr, gr, gn_w):
    B, S, _ = qr.shape
    C = RET_C
    H = RET_HEADS
    log_gamma = jnp.log1p(-jnp.exp2(-5.0 - jnp.arange(H, dtype=F32)))
    idx = jnp.arange(C, dtype=F32)
    diff = idx[:, None] - idx[None, :]
    din = jnp.where(diff >= 0, jnp.exp(log_gamma[:, None, None] * jnp.maximum(diff, 0.0)), 0.0)
    qd = jnp.exp(log_gamma[:, None] * (idx[None, :] + 1.0))
    qd = jnp.broadcast_to(qd[:, :, None], (H, C, RET_DV))
    kd = jnp.exp(log_gamma[:, None] * (C - 1.0 - idx[None, :]))[:, None, :]
    cd = jnp.broadcast_to(jnp.exp(log_gamma * C)[:, None, None], (H, 1, RET_DV))
    all_h = lambda *shape: pl.BlockSpec((H,) + shape, lambda b, n: (0, 0, 0))
    rows = lambda w: pl.BlockSpec((1, C, w), lambda b, n: (b, n, 0))
    return pl.pallas_call(
        _ret_kernel,
        grid=(B, S // C),
        in_specs=[
            rows(H * RET_DK),
            pl.BlockSpec((1, H, RET_DK, C), lambda b, n: (b, 0, 0, n)),
            rows(H * RET_DV),
            rows(H * RET_DV),
            all_h(C, C), all_h(C, RET_DV), all_h(1, C), all_h(1, RET_DV), all_h(1, RET_DV),
        ],
        out_specs=rows(H * RET_DV),
        out_shape=jax.ShapeDtypeStruct((B, S, H * RET_DV), BF16),
        scratch_shapes=[pltpu.VMEM((H, RET_DK, RET_DV), F32)],
        compiler_params=_cparams(("arbitrary", "arbitrary")),
        name="retention",
    )(qr, krT, vr, gr, din, qd, kd, cd, gn_w.reshape(H, 1, RET_DV))


def _mix_kernel(x_ref, nw_ref, oa_ref, or_ref, wg_ref, wn_ref, wr_ref, wm_ref, o_ref):
    d = x_ref.shape[2]
    x = x_ref[0]
    hb = _rms(x, nw_ref[...]).astype(BF16)
    gate = jax.nn.sigmoid(jnp.dot(hb, wg_ref[...], preferred_element_type=F32))
    kw = NSA_G * DH
    y_a = jnp.dot(oa_ref[0, 0], wn_ref[0:kw, :], preferred_element_type=F32)
    for h in range(1, NSA_KV):
        y_a = y_a + jnp.dot(oa_ref[0, h], wn_ref[h * kw:(h + 1) * kw, :], preferred_element_type=F32)
    y_b = jnp.dot(or_ref[0], wr_ref[...], preferred_element_type=F32)
    mix = (gate[:, :d] * y_a + gate[:, d:] * y_b).astype(BF16)
    o_ref[0] = x + jnp.dot(mix, wm_ref[...], preferred_element_type=F32)


def _mix(x, norm_w, oa, o_r, wg, wn, wr, wm):
    B, S, D = x.shape
    tm = MIX_TM
    row = lambda b, i: (b, i, 0)
    full = lambda a: pl.BlockSpec(a.shape, lambda b, i: (0,) * a.ndim)
    nw = norm_w.reshape(1, D)
    return pl.pallas_call(
        _mix_kernel,
        grid=(B, S // tm),
        in_specs=[
            pl.BlockSpec((1, tm, D), row), full(nw),
            pl.BlockSpec((1, NSA_KV, tm, NSA_G * DH), lambda b, i: (b, 0, i, 0)),
            pl.BlockSpec((1, tm, RET_HEADS * RET_DV), row),
            full(wg), full(wn), full(wr), full(wm),
        ],
        out_specs=pl.BlockSpec((1, tm, D), row),
        out_shape=jax.ShapeDtypeStruct((B, S, D), F32),
        compiler_params=_cparams(("arbitrary", "arbitrary")),
        name="mixer_merge",
    )(x, nw, oa, o_r, wg, wn, wr, wm)


def _ffn_kernel(x_ref, halo_ref, nw_ref, wu_ref, cw_ref, cb_ref, wd_ref, fw_ref, o_ref,
                hb_ref, u_a, u_b, acc_ref, *, final_norm):
    tm = x_ref.shape[1]
    tf = wd_ref.shape[1]
    nt = wu_ref.shape[0]
    x = x_ref[0]
    hb_ref[:HALO, :] = jnp.where(pl.program_id(1) > 0, _rms(halo_ref[0], nw_ref[...]), 0.0).astype(BF16)
    hb_ref[HALO:, :] = _rms(x, nw_ref[...]).astype(BF16)
    acc_ref[...] = jnp.zeros_like(acc_ref)
    slots = (u_a, u_b)

    def up(j, slot):
        slots[slot][...] = jnp.dot(hb_ref[...], wu_ref[j], preferred_element_type=F32)

    def down(j, slot):
        u_ref = slots[slot]
        cw = cw_ref[j]
        y = cb_ref[j]
        for k in range(CONV_WIDTH):
            lo = HALO - (CONV_WIDTH - 1) + k
            y = y + cw[k:k + 1, :] * u_ref[lo:lo + tm, :]
        act = (jax.nn.silu(y[:, :tf]) * y[:, tf:]).astype(BF16)
        acc_ref[...] += jnp.dot(act, wd_ref[j], preferred_element_type=F32)

    up(0, 0)

    def pair(p, carry):
        up(2 * p + 1, 1)
        down(2 * p, 0)
        up(2 * p + 2, 0)
        down(2 * p + 1, 1)
        return carry

    lax.fori_loop(0, (nt - 1) // 2, pair, 0)
    if (nt - 1) % 2 == 1:
        up(nt - 1, 1)
        down(nt - 2, 0)
        down(nt - 1, 1)
    else:
        down(nt - 1, 0)
    y = x + acc_ref[...]
    o_ref[0] = _rms(y, fw_ref[...]) if final_norm else y


def _ffn(x, norm_w, w_up, conv_w, conv_b, w_down, final_w, final_norm):
    B, S, D = x.shape
    tm, tf = FFN_TM, FFN_TF
    dff = w_down.shape[0]
    nt = dff // tf
    pair = lambda a: jnp.concatenate([a[..., :dff].reshape(a.shape[:-1] + (nt, tf)),
                                      a[..., dff:].reshape(a.shape[:-1] + (nt, tf))], axis=-1)
    wu = jnp.swapaxes(pair(w_up.astype(BF16)), 0, 1)
    cw = jnp.swapaxes(pair(conv_w), 0, 1)
    cb = pair(conv_b)[:, None, :]
    wd = w_down.astype(BF16).reshape(nt, tf, D)
    row = lambda b, i: (b, i, 0)
    full = lambda a: pl.BlockSpec(a.shape, lambda b, i: (0,) * a.ndim)
    nw = norm_w.reshape(1, D)
    fw = final_w.reshape(1, D)
    return pl.pallas_call(
        functools.partial(_ffn_kernel, final_norm=final_norm),
        grid=(B, S // tm),
        in_specs=[
            pl.BlockSpec((1, tm, D), row),
            pl.BlockSpec((1, HALO, D), lambda b, i: (b, jnp.maximum(i * (tm // HALO) - 1, 0), 0)),
            full(nw), full(wu), full(cw), full(cb), full(wd), full(fw),
        ],
        out_specs=pl.BlockSpec((1, tm, D), row),
        out_shape=jax.ShapeDtypeStruct((B, S, D), F32),
        scratch_shapes=[
            pltpu.VMEM((HALO + tm, D), BF16),
            pltpu.VMEM((HALO + tm, 2 * tf), F32),
            pltpu.VMEM((HALO + tm, 2 * tf), F32),
            pltpu.VMEM((tm, D), F32),
        ],
        compiler_params=_cparams(("arbitrary", "arbitrary")),
        name="conv_ffn",
    )(x, x, nw, wu, cw, cb, wd, fw)


def kernel(x, norm_mix_w, w_in, cmp_pe_k, cmp_w1_k, cmp_w2_k, cmp_pe_v, cmp_w1_v, cmp_w2_v, w_nsa_branch, ret_gn_w, w_ret_branch, w_mix_out, norm_ffn_w, w_ffn_up, ffn_conv_w, ffn_conv_b, w_ffn_down, norm_final_w):
    depth = w_in.shape[0]
    d = x.shape[-1]
    for layer in range(depth):
        qT, kvc, kaug, kwin, vsT, vwT, gT, qr, krT, vr, gr = _projection(x, norm_mix_w[layer], w_in[layer])
        kc, vcT = _compress(kvc, cmp_pe_k[layer], cmp_w1_k[layer], cmp_w2_k[layer],
                            cmp_pe_v[layer], cmp_w1_v[layer], cmp_w2_v[layer])
        oa = _attention(qT, kaug, vsT, kwin, vwT, kc, vcT, gT)
        o_r = _retention(qr, krT, vr, gr, ret_gn_w[layer])
        wg = w_in[layer][:, -2 * d:].astype(BF16)
        x1 = _mix(x, norm_mix_w[layer], oa, o_r, wg, w_nsa_branch[layer].astype(BF16),
                  w_ret_branch[layer].astype(BF16), w_mix_out[layer].astype(BF16))
        x = _ffn(x1, norm_ffn_w[layer], w_ffn_up[layer], ffn_conv_w[layer], ffn_conv_b[layer],
                 w_ffn_down[layer], norm_final_w, final_norm=layer == depth - 1)
    return x
```

```python
import functools

import numpy as np
import jax
import jax.numpy as jnp
from jax import lax
from jax.experimental import pallas as pl
from jax.experimental.pallas import tpu as pltpu

F32 = jnp.float32
BF16 = jnp.bfloat16

NSA_HEADS = 8
NSA_KV = 2
NSA_G = NSA_HEADS // NSA_KV
DH = 64
ROPE_DIM = DH // 4
ROPE_THETA = 500000.0
CMP_BLOCK = 32
CMP_STRIDE = 16
CMP_HIDDEN = 256
SLC_BLOCK = 64
SLC_TOPK = 16
WINDOW = 512
RET_HEADS = 4
RET_DK = 128
RET_DV = 256
RET_THETA = 10000.0
CONV_WIDTH = 3
EPS = 1e-6
NEG = -1e30
LOG2E = 1.4426950408889634

LANES = 128
VMEM_LIMIT = 56 * 1024 * 1024

PROJ_TM = 256
ATT_SQ = 256
ATT_CH = 512
ONES_ROWS = 16
RET_C = 256
MIX_TM = 512
FFN_TM = 512
FFN_TF = 256
HALO = 16


def _cparams(sem):
    return pltpu.CompilerParams(dimension_semantics=sem, vmem_limit_bytes=VMEM_LIMIT)


def _rms(x, w):
    return x * lax.rsqrt(jnp.mean(x * x, axis=-1, keepdims=True) + EPS) * w


_C_Q = (0, 512)
_C_KVC = (512, 768)
_C_KSLC = (768, 1024)
_C_KWIN = (1024, 1280)
_C_VSW = (1280, 1536)
_C_G = (1536, 1664)
_C_QR = (1664, 2176)
_C_KR = (2176, 2688)
_C_VR = (2688, 3712)
_C_GR = (3712, 4736)
_PROJ_COLS = 4736


def _proj_kernel(x_ref, nw_ref, w_ref, cn_ref, s1_ref, s2_ref, cr_ref, sr_ref,
                 qT_ref, kvc_ref, kslc_ref, kwin_ref, vsT_ref, vwT_ref, gT_ref,
                 qr_ref, krT_ref, vr_ref, gr_ref, *, seq):
    tm = x_ref.shape[1]
    i = pl.program_id(1)
    hb = _rms(x_ref[0], nw_ref[...]).astype(BF16)

    def mm(c):
        return jnp.dot(hb, w_ref[:, c[0]:c[1]], preferred_element_type=F32)

    cn, s1, s2 = cn_ref[...], s1_ref[...], s2_ref[...]

    def rope_nsa(t):
        return t * cn + pltpu.roll(t, 8, 1) * s1 + pltpu.roll(t, LANES - 8, 1) * s2

    acc = mm(_C_Q)
    for j in range(4):
        t = rope_nsa(acc[:, j * LANES:(j + 1) * LANES]) * (DH ** -0.5 * LOG2E)
        qT_ref[0, 2 * j:2 * j + 2] = t.T.astype(BF16).reshape(2, DH, tm)

    acc = mm(_C_KVC)
    for kind in range(2):
        kvc_ref[0, kind] = acc[:, kind * LANES:(kind + 1) * LANES]

    lane = lax.broadcasted_iota(jnp.int32, (tm, LANES), 1)
    pos = i * tm + lax.broadcasted_iota(jnp.int32, (tm, LANES), 0)
    onehot = jnp.where(lane - DH == pos // SLC_BLOCK, 1.0, 0.0)
    acc = mm(_C_KSLC)
    for h in range(NSA_KV):
        t = rope_nsa(acc[:, h * LANES:(h + 1) * LANES])
        kslc_ref[0, h] = jnp.where(lane >= DH, onehot, t).astype(BF16)
    acc = mm(_C_KWIN)
    for h in range(NSA_KV):
        kwin_ref[0, h] = rope_nsa(acc[:, h * LANES:(h + 1) * LANES]).astype(BF16)

    acc = mm(_C_VSW)
    vsT_ref[0] = acc[:, :LANES].T.astype(BF16).reshape(NSA_KV, DH, tm)
    vwT_ref[0] = acc[:, LANES:].T.astype(BF16).reshape(NSA_KV, DH, tm)

    g = jax.nn.sigmoid(mm(_C_G)).T
    gT_ref[0] = g[:32].reshape(NSA_KV, 16, tm)

    cr, sr = cr_ref[...], sr_ref[...]

    def rope_ret(t):
        return t * cr + pltpu.roll(t, RET_DK // 2, 1) * sr

    acc = mm(_C_QR)
    for j in range(RET_HEADS):
        qr_ref[0, :, j * LANES:(j + 1) * LANES] = rope_ret(acc[:, j * LANES:(j + 1) * LANES]).astype(BF16)
    acc = mm(_C_KR)
    for j in range(RET_HEADS):
        t = rope_ret(acc[:, j * LANES:(j + 1) * LANES]) * (RET_DK ** -0.5)
        krT_ref[0, j] = t.T.astype(BF16)
    vr_ref[0] = mm(_C_VR).astype(BF16)
    gr_ref[0] = mm(_C_GR).astype(BF16)


def _pack_proj_weight(w_in):
    w = w_in.astype(BF16)
    d = w.shape[0]
    z64 = jnp.zeros((d, DH), BF16)
    kv0 = NSA_HEADS * DH

    def kv(r, h):
        c = kv0 + r * NSA_KV * DH + h * DH
        return w[:, c:c + DH]

    g0 = kv0 + 6 * NSA_KV * DH
    gidx = []
    for h in range(NSA_KV):
        for r in range(3):
            for g in range(NSA_G):
                gidx.append(g0 + (h * NSA_G + g) * 3 + r)
        gidx += [g0] * 4
    gcols = w[:, np.asarray(gidx)]
    r0 = g0 + 3 * NSA_HEADS
    parts = [
        w[:, :kv0],
        w[:, kv0:kv0 + 2 * NSA_KV * DH],
        kv(2, 0), z64, kv(2, 1), z64,
        kv(4, 0), z64, kv(4, 1), z64,
        kv(3, 0), kv(3, 1), kv(5, 0), kv(5, 1),
        gcols, jnp.zeros((d, LANES - 32), BF16),
        w[:, r0:r0 + 2 * RET_HEADS * RET_DK + 2 * RET_HEADS * RET_DV],
    ]
    out = jnp.concatenate(parts, axis=1)
    assert out.shape[1] == _PROJ_COLS
    return out


def _rope_tables(seq):
    pos = jnp.arange(seq, dtype=F32)[:, None]
    half = ROPE_DIM // 2
    inv = jnp.power(jnp.float32(ROPE_THETA), -jnp.arange(half, dtype=F32) / half)
    inv_head = jnp.concatenate([inv, inv, jnp.zeros((DH - ROPE_DIM,), F32)])
    ang = pos * jnp.concatenate([inv_head, inv_head])[None, :]
    d = np.arange(LANES) % DH
    cn = jnp.cos(ang)
    s1 = jnp.sin(ang) * jnp.asarray((d >= half) & (d < ROPE_DIM), F32)
    s2 = jnp.sin(ang) * jnp.asarray(-1.0 * (d < half), F32)
    half_r = RET_DK // 2
    inv_r = jnp.power(jnp.float32(RET_THETA), -jnp.arange(half_r, dtype=F32) / half_r)
    ang_r = pos * jnp.concatenate([inv_r, inv_r])[None, :]
    cr = jnp.cos(ang_r)
    sr = jnp.sin(ang_r) * jnp.asarray(np.where(np.arange(RET_DK) < half_r, -1.0, 1.0), F32)
    return cn, s1, s2, cr, sr


def _projection(x, norm_w, w_in):
    B, S, D = x.shape
    tm = PROJ_TM
    wp = _pack_proj_weight(w_in)
    tabs = _rope_tables(S)
    row = lambda b, i: (b, i, 0)
    tab_spec = pl.BlockSpec((tm, LANES), lambda b, i: (i, 0))
    out_shape = (
        jax.ShapeDtypeStruct((B, NSA_HEADS, DH, S), BF16),
        jax.ShapeDtypeStruct((B, 2, S, NSA_KV * DH), F32),
        jax.ShapeDtypeStruct((B, NSA_KV, S, LANES), BF16),
        jax.ShapeDtypeStruct((B, NSA_KV, S, LANES), BF16),
        jax.ShapeDtypeStruct((B, NSA_KV, DH, S), BF16),
        jax.ShapeDtypeStruct((B, NSA_KV, DH, S), BF16),
        jax.ShapeDtypeStruct((B, NSA_KV, 16, S), F32),
        jax.ShapeDtypeStruct((B, S, RET_HEADS * RET_DK), BF16),
        jax.ShapeDtypeStruct((B, RET_HEADS, RET_DK, S), BF16),
        jax.ShapeDtypeStruct((B, S, RET_HEADS * RET_DV), BF16),
        jax.ShapeDtypeStruct((B, S, RET_HEADS * RET_DV), BF16),
    )
    tok_last = lambda n, d: pl.BlockSpec((1, n, d, tm), lambda b, i: (b, 0, 0, i))
    out_specs = (
        tok_last(NSA_HEADS, DH),
        pl.BlockSpec((1, 2, tm, NSA_KV * DH), lambda b, i: (b, 0, i, 0)),
        pl.BlockSpec((1, NSA_KV, tm, LANES), lambda b, i: (b, 0, i, 0)),
        pl.BlockSpec((1, NSA_KV, tm, LANES), lambda b, i: (b, 0, i, 0)),
        tok_last(NSA_KV, DH),
        tok_last(NSA_KV, DH),
        tok_last(NSA_KV, 16),
        pl.BlockSpec((1, tm, RET_HEADS * RET_DK), row),
        tok_last(RET_HEADS, RET_DK),
        pl.BlockSpec((1, tm, RET_HEADS * RET_DV), row),
        pl.BlockSpec((1, tm, RET_HEADS * RET_DV), row),
    )
    return pl.pallas_call(
        functools.partial(_proj_kernel, seq=S),
        grid=(B, S // tm),
        in_specs=[
            pl.BlockSpec((1, tm, D), row),
            pl.BlockSpec((1, D), lambda b, i: (0, 0)),
            pl.BlockSpec((D, _PROJ_COLS), lambda b, i: (0, 0)),
            tab_spec, tab_spec, tab_spec, tab_spec, tab_spec,
        ],
        out_specs=out_specs,
        out_shape=out_shape,
        compiler_params=_cparams(("arbitrary", "arbitrary")),
        name="in_proj",
    )(x, norm_w.reshape(1, D), wp, *tabs)


def _cmp_kernel(kvc_ref, pek_ref, pev_ref, w1k_ref, w1v_ref, w2k_ref, w2v_ref, kc_ref, vcT_ref):
    nseg = kvc_ref.shape[2] // CMP_STRIDE
    half = CMP_BLOCK // 2
    row = lax.broadcasted_iota(jnp.int32, (nseg, 1), 0)
    for kind, (pe_ref, w1_ref, w2_ref) in enumerate(((pek_ref, w1k_ref, w2k_ref), (pev_ref, w1v_ref, w2v_ref))):
        acc_a = [jnp.zeros((nseg, CMP_HIDDEN), F32) for _ in range(NSA_KV)]
        acc_b = [jnp.zeros((nseg, CMP_HIDDEN), F32) for _ in range(NSA_KV)]
        for l in range(half):
            xl = kvc_ref[0, kind, pl.ds(l, nseg, stride=CMP_STRIDE), :]
            xa = (xl + pe_ref[l:l + 1, :]).astype(BF16)
            xb = (xl + pe_ref[half + l:half + l + 1, :]).astype(BF16)
            for h in range(NSA_KV):
                acc_a[h] += jnp.dot(xa[:, h * DH:(h + 1) * DH], w1_ref[l], preferred_element_type=F32)
                acc_b[h] += jnp.dot(xb[:, h * DH:(h + 1) * DH], w1_ref[half + l], preferred_element_type=F32)
        for h in range(NSA_KV):
            nxt = jnp.concatenate([acc_b[h][1:], jnp.zeros((1, CMP_HIDDEN), F32)], axis=0)
            hid = jax.nn.gelu(acc_a[h] + nxt).astype(BF16)
            out = jnp.dot(hid, w2_ref[...], preferred_element_type=F32)
            out = jnp.where(row < nseg - 1, out, 0.0)
            if kind == 0:
                kc_ref[0, h] = jnp.concatenate([out, jnp.zeros((nseg, LANES - DH), F32)], axis=1).astype(BF16)
            else:
                vcT_ref[0, h] = out.T.astype(BF16)


def _compress(kvc, pe_k, w1_k, w2_k, pe_v, w1_v, w2_v):
    B, _, S, _ = kvc.shape
    nseg = S // CMP_STRIDE
    tile2 = lambda t: jnp.concatenate([t, t], axis=1)
    w1 = lambda w: w.astype(BF16).reshape(CMP_BLOCK, DH, CMP_HIDDEN)
    full = lambda *shape: pl.BlockSpec(shape, lambda b: (0,) * len(shape))
    return pl.pallas_call(
        _cmp_kernel,
        grid=(B,),
        in_specs=[
            pl.BlockSpec((1, 2, S, NSA_KV * DH), lambda b: (b, 0, 0, 0)),
            full(CMP_BLOCK, LANES), full(CMP_BLOCK, LANES),
            full(CMP_BLOCK, DH, CMP_HIDDEN), full(CMP_BLOCK, DH, CMP_HIDDEN),
            full(CMP_HIDDEN, DH), full(CMP_HIDDEN, DH),
        ],
        out_specs=(
            pl.BlockSpec((1, NSA_KV, nseg, LANES), lambda b: (b, 0, 0, 0)),
            pl.BlockSpec((1, NSA_KV, DH, nseg), lambda b: (b, 0, 0, 0)),
        ),
        out_shape=(
            jax.ShapeDtypeStruct((B, NSA_KV, nseg, LANES), BF16),
            jax.ShapeDtypeStruct((B, NSA_KV, DH, nseg), BF16),
        ),
        compiler_params=_cparams(("arbitrary",)),
        name="nsa_compress",
    )(kvc, tile2(pe_k), tile2(pe_v), w1(w1_k), w1(w1_v), w2_k.astype(BF16), w2_v.astype(BF16))


def _overlap_matrix(seq):
    nseg = seq // CMP_STRIDE
    n_slc = seq // SLC_BLOCK
    cs = np.arange(nseg) * CMP_STRIDE
    ss = np.arange(n_slc) * SLC_BLOCK
    ov = np.clip(np.minimum(cs[None, :] + CMP_BLOCK, ss[:, None] + SLC_BLOCK)
                 - np.maximum(cs[None, :], ss[:, None]), 0, None) / CMP_BLOCK
    ov[:, nseg - 1] = 0.0
    return ov


def _with_ones(vT):
    return jnp.concatenate([vT, jnp.ones((ONES_ROWS, vT.shape[1]), BF16)], axis=0)


def _att_kernel(qT_ref, ka_ref, vsT_ref, kw_ref, vwT_ref, kc_ref, vcT_ref, g_ref, ov_ref, wb_ref, cb_ref, o_ref,
                s_a, s_b, cm_a, cm_b, m_scr, acc_scr, sw_scr, imp_scr, rank_scr):
    sq = qT_ref.shape[3]
    nq4 = NSA_G * sq
    n_slc = ov_ref.shape[0]
    nseg = kc_ref.shape[2]
    i = pl.program_id(2)
    q0 = pl.multiple_of(i * sq, sq)

    qT4 = jnp.concatenate([qT_ref[0, g] for g in range(NSA_G)], axis=1)
    qw = jnp.concatenate([qT4, jnp.zeros((LANES - DH, nq4), BF16)], axis=0)
    tq = q0 + lax.broadcasted_iota(jnp.int32, (1, sq), 1)

    def band_bias(key0, n_tiles, width):
        dq = (q0 - key0) // sq
        tiles = []
        for a in range(n_tiles):
            delta = dq - a
            kind = jnp.where(delta == 0, 1, jnp.where(delta == width, 3, 2))
            kind = jnp.where(jnp.logical_or(delta < 0, delta > width), 0, kind)
            tiles.append(jnp.concatenate([wb_ref[kind]] * NSA_G, axis=1))
        return jnp.concatenate(tiles, axis=0)

    nk = WINDOW + sq
    ks = pl.multiple_of(jnp.maximum(q0 - WINDOW, 0), sq)
    s = jnp.dot(kw_ref[0, 0, pl.ds(ks, nk), :], qw, preferred_element_type=F32)
    s = s + band_bias(ks, nk // sq, WINDOW // sq)
    sw_scr[...] = s
    m_win = jnp.max(s, axis=0, keepdims=True)

    s = jnp.dot(kc_ref[0, 0], qw, preferred_element_type=F32)
    cb = cb_ref[pl.ds(pl.multiple_of(nseg - q0 // CMP_STRIDE, CMP_STRIDE), nseg), :]
    s = s + jnp.concatenate([cb] * NSA_G, axis=1)
    m = jnp.maximum(jnp.max(s, axis=0, keepdims=True), 0.1 * NEG)
    p = jnp.exp2(s - m)
    l = jnp.sum(p, axis=0, keepdims=True)
    pn = p * jnp.where(l > 0.0, 1.0 / l, 0.0)
    o_cmp = jnp.dot(vcT_ref[0, 0], pn.astype(BF16), preferred_element_type=F32)

    psum = pn[:, 0:sq]
    for g in range(1, NSA_G):
        psum = psum + pn[:, g * sq:(g + 1) * sq]
    p_hi = psum.astype(BF16)
    p_lo = (psum - p_hi.astype(F32)).astype(BF16)
    imp = (jnp.dot(ov_ref[...], p_hi, preferred_element_type=F32)
           + jnp.dot(ov_ref[...], p_lo, preferred_element_type=F32))
    jb = lax.broadcasted_iota(jnp.int32, (n_slc, 1), 0)
    cur = tq // SLC_BLOCK
    forced = jnp.logical_or(jb == 0, jnp.logical_or(jb == cur, jb == cur - 1))
    imp = jnp.where(forced, jnp.inf, jnp.where(jb <= cur, imp, -jnp.inf))
    imp_scr[...] = imp
    rank_scr[...] = jnp.zeros_like(rank_scr)
    sub = lax.broadcasted_iota(jnp.int32, (8, 1), 0)
    g_last = ((q0 + sq - 1) // SLC_BLOCK) // 8

    def count(vp, v):
        x = imp_scr[8 * v:8 * v + 8, :]
        acc = rank_scr[8 * v:8 * v + 8, :]
        for jp in range(8 * vp, 8 * vp + 8):
            r = imp_scr[jp:jp + 1, :]
            if vp < v:
                cnt = jnp.where(r >= x, 1.0, 0.0)
            elif vp > v:
                cnt = jnp.where(r > x, 1.0, 0.0)
            else:
                cnt = jnp.where(sub > jp - 8 * v, jnp.where(r >= x, 1.0, 0.0), jnp.where(r > x, 1.0, 0.0))
            acc = acc + cnt
        rank_scr[8 * v:8 * v + 8, :] = acc

    for g in range(n_slc // 8):
        @pl.when(g <= g_last)
        def _():
            for v in range(g + 1):
                count(g, v)
            for vp in range(g):
                count(vp, g)

    bias = jnp.where(rank_scr[...] < float(min(SLC_TOPK, n_slc)), 0.0, NEG).astype(BF16)
    if n_slc < LANES - DH:
        bias = jnp.concatenate([bias, jnp.zeros((LANES - DH - n_slc, sq), BF16)], axis=0)
    qa = jnp.concatenate([qT4, jnp.concatenate([bias] * NSA_G, axis=1)], axis=0)

    ch = s_a.shape[0]
    last = (i * sq) // ch
    slots = ((s_a, cm_a), (s_b, cm_b))

    def produce(c, slot, causal=False):
        s_ref, cm_ref = slots[slot]
        k0 = pl.multiple_of(c * ch, ch)
        s = jnp.dot(ka_ref[0, 0, pl.ds(k0, ch), :], qa, preferred_element_type=F32)
        if causal:
            s = s + band_bias(k0, ch // sq, nseg * CMP_STRIDE // sq)
        s_ref[...] = s
        cm_ref[...] = jnp.max(s, axis=0, keepdims=True)

    def consume(c, slot):
        s_ref, cm_ref = slots[slot]
        k0 = pl.multiple_of(c * ch, ch)
        m_old = m_scr[...]
        m_new = jnp.maximum(m_old, cm_ref[...])
        p = jnp.exp2(s_ref[...] - m_new).astype(BF16)
        pv = jnp.dot(_with_ones(vsT_ref[0, 0, :, pl.ds(k0, ch)]), p, preferred_element_type=F32)
        acc_scr[...] = jnp.exp2(m_old - m_new) * acc_scr[...] + pv
        m_scr[...] = m_new

    m_scr[...] = jnp.full_like(m_scr, -jnp.inf)
    acc_scr[...] = jnp.zeros_like(acc_scr)
    produce(last, 0, causal=True)

    p = jnp.exp2(sw_scr[...] - m_win).astype(BF16)
    r = jnp.dot(_with_ones(vwT_ref[0, 0, :, pl.ds(ks, nk)]), p, preferred_element_type=F32)
    o_win = r[:DH] * (1.0 / r[DH:DH + 1])

    def pair(j, carry):
        produce(2 * j, 1)
        consume(jnp.where(j == 0, last, 2 * j - 1), 0)
        produce(2 * j + 1, 0)
        consume(2 * j, 1)
        return carry

    npairs = last // 2
    lax.fori_loop(0, npairs, pair, 0)
    held = jnp.where(npairs == 0, last, 2 * npairs - 1)

    @pl.when(last % 2 == 1)
    def _():
        produce(last - 1, 1)
        consume(held, 0)
        consume(last - 1, 1)

    @pl.when(last % 2 == 0)
    def _():
        consume(held, 0)

    o_slc = acc_scr[:DH] * (1.0 / acc_scr[DH:DH + 1])

    def gate(r):
        return jnp.concatenate([g_ref[0, 0, r * NSA_G + g:r * NSA_G + g + 1, :] for g in range(NSA_G)], axis=1)

    o = gate(0) * o_cmp + gate(1) * o_slc + gate(2) * o_win
    o = jnp.concatenate([o[:, g * sq:(g + 1) * sq] for g in range(NSA_G)], axis=0)
    o_ref[0, 0] = o.T.astype(BF16)


def _attention(qT, kaug, vsT, kwin, vwT, kc, vcT, gT):
    B, _, _, S = qT.shape
    sq = ATT_SQ
    nseg = S // CMP_STRIDE
    n_slc = S // SLC_BLOCK
    ov = jnp.asarray(_overlap_matrix(S), BF16)
    rr = np.arange(sq)[:, None]
    tl = np.arange(sq)[None, :]
    vis = np.stack([np.zeros((sq, sq), bool), rr <= tl, np.ones((sq, sq), bool), rr > tl])
    wb = jnp.asarray(np.where(vis, 0.0, NEG), F32)
    cj = np.arange(2 * nseg)[:, None] - nseg
    cb = jnp.asarray(np.where(CMP_STRIDE * cj + (CMP_BLOCK - 1) <= tl, 0.0, NEG), F32)
    per_bh = lambda *shape: pl.BlockSpec((1, 1) + shape, lambda b, h, i: (b, h, 0, 0))
    return pl.pallas_call(
        _att_kernel,
        grid=(B, NSA_KV, S // sq),
        in_specs=[
            pl.BlockSpec((1, NSA_G, DH, sq), lambda b, h, i: (b, h, 0, i)),
            per_bh(S, LANES), per_bh(DH, S), per_bh(S, LANES), per_bh(DH, S),
            per_bh(nseg, LANES), per_bh(DH, nseg),
            pl.BlockSpec((1, 1, 16, sq), lambda b, h, i: (b, h, 0, i)),
            pl.BlockSpec((n_slc, nseg), lambda b, h, i: (0, 0)),
            pl.BlockSpec((4, sq, sq), lambda b, h, i: (0, 0, 0)),
            pl.BlockSpec((2 * nseg, sq), lambda b, h, i: (0, 0)),
        ],
        out_specs=pl.BlockSpec((1, 1, sq, NSA_G * DH), lambda b, h, i: (b, h, i, 0)),
        out_shape=jax.ShapeDtypeStruct((B, NSA_KV, S, NSA_G * DH), BF16),
        scratch_shapes=[
            pltpu.VMEM((ATT_CH, NSA_G * sq), F32),
            pltpu.VMEM((ATT_CH, NSA_G * sq), F32),
            pltpu.VMEM((1, NSA_G * sq), F32),
            pltpu.VMEM((1, NSA_G * sq), F32),
            pltpu.VMEM((1, NSA_G * sq), F32),
            pltpu.VMEM((DH + ONES_ROWS, NSA_G * sq), F32),
            pltpu.VMEM((WINDOW + sq, NSA_G * sq), F32),
            pltpu.VMEM((n_slc, sq), F32),
            pltpu.VMEM((n_slc, sq), F32),
        ],
        compiler_params=_cparams(("arbitrary", "arbitrary", "arbitrary")),
        name="nsa_attention",
    )(qT, kaug, vsT, kwin, vwT, kc, vcT, gT, ov, wb, cb)


def _ret_kernel(q_ref, kT_ref, v_ref, g_ref, din_ref, qd_ref, kd_ref, cd_ref, gn_ref, o_ref, state_ref):
    @pl.when(pl.program_id(1) == 0)
    def _():
        state_ref[...] = jnp.zeros_like(state_ref)

    for h in range(RET_HEADS):
        q = q_ref[0, :, h * RET_DK:(h + 1) * RET_DK]
        kT = kT_ref[0, h]
        v = v_ref[0, :, h * RET_DV:(h + 1) * RET_DV]
        state = state_ref[h]
        sc = jnp.dot(q, kT, preferred_element_type=F32) * din_ref[h]
        o = jnp.dot(sc.astype(BF16), v, preferred_element_type=F32)
        o = o + jnp.dot(q, state.astype(BF16), preferred_element_type=F32) * qd_ref[h]
        kd = (kT.astype(F32) * kd_ref[h]).astype(BF16)
        state_ref[h] = state * cd_ref[h] + jnp.dot(kd, v, preferred_element_type=F32)

        mean = jnp.mean(o, axis=-1, keepdims=True)
        var = jnp.mean(jnp.square(o - mean), axis=-1, keepdims=True)
        y = (o - mean) * lax.rsqrt(var + EPS) * gn_ref[h]
        g = g_ref[0, :, h * RET_DV:(h + 1) * RET_DV].astype(F32)
        o_ref[0, :, h * RET_DV:(h + 1) * RET_DV] = (y * jax.nn.silu(g)).astype(BF16)


def _retention(qr, krT, vr, gr, gn_w):
    B, S, _ = qr.shape
    C = RET_C
    H = RET_HEADS
    log_gamma = jnp.log1p(-jnp.exp2(-5.0 - jnp.arange(H, dtype=F32)))
    idx = jnp.arange(C, dtype=F32)
    diff = idx[:, None] - idx[None, :]
    din = jnp.where(diff >= 0, jnp.exp(log_gamma[:, None, None] * jnp.maximum(diff, 0.0)), 0.0)
    qd = jnp.exp(log_gamma[:, None] * (idx[None, :] + 1.0))
    qd = jnp.broadcast_to(qd[:, :, None], (H, C, RET_DV))
    kd = jnp.exp(log_gamma[:, None] * (C - 1.0 - idx[None, :]))[:, None, :]
    cd = jnp.broadcast_to(jnp.exp(log_gamma * C)[:, None, None], (H, 1, RET_DV))
    all_h = lambda *shape: pl.BlockSpec((H,) + shape, lambda b, n: (0, 0, 0))
    rows = lambda w: pl.BlockSpec((1, C, w), lambda b, n: (b, n, 0))
    return pl.pallas_call(
        _ret_kernel,
        grid=(B, S // C),
        in_specs=[
            rows(H * RET_DK),
            pl.BlockSpec((1, H, RET_DK, C), lambda b, n: (b, 0, 0, n)),
            rows(H * RET_DV),
            rows(H * RET_DV),
            all_h(C, C), all_h(C, RET_DV), all_h(1, C), all_h(1, RET_DV), all_h(1, RET_DV),
        ],
        out_specs=rows(H * RET_DV),
        out_shape=jax.ShapeDtypeStruct((B, S, H * RET_DV), BF16),
        scratch_shapes=[pltpu.VMEM((H, RET_DK, RET_DV), F32)],
        compiler_params=_cparams(("arbitrary", "arbitrary")),
        name="retention",
    )(qr, krT, vr, gr, din, qd, kd, cd, gn_w.reshape(H, 1, RET_DV))


def _mix_kernel(x_ref, nw_ref, oa_ref, or_ref, wg_ref, wn_ref, wr_ref, wm_ref, o_ref):
    d = x_ref.shape[2]
    x = x_ref[0]
    hb = _rms(x, nw_ref[...]).astype(BF16)
    gate = jax.nn.sigmoid(jnp.dot(hb, wg_ref[...], preferred_element_type=F32))
    kw = NSA_G * DH
    y_a = jnp.dot(oa_ref[0, 0], wn_ref[0:kw, :], preferred_element_type=F32)
    for h in range(1, NSA_KV):
        y_a = y_a + jnp.dot(oa_ref[0, h], wn_ref[h * kw:(h + 1) * kw, :], preferred_element_type=F32)
    y_b = jnp.dot(or_ref[0], wr_ref[...], preferred_element_type=F32)
    mix = (gate[:, :d] * y_a + gate[:, d:] * y_b).astype(BF16)
    o_ref[0] = x + jnp.dot(mix, wm_ref[...], preferred_element_type=F32)


def _mix(x, norm_w, oa, o_r, wg, wn, wr, wm):
    B, S, D = x.shape
    tm = MIX_TM
    row = lambda b, i: (b, i, 0)
    full = lambda a: pl.BlockSpec(a.shape, lambda b, i: (0,) * a.ndim)
    nw = norm_w.reshape(1, D)
    return pl.pallas_call(
        _mix_kernel,
        grid=(B, S // tm),
        in_specs=[
            pl.BlockSpec((1, tm, D), row), full(nw),
            pl.BlockSpec((1, NSA_KV, tm, NSA_G * DH), lambda b, i: (b, 0, i, 0)),
            pl.BlockSpec((1, tm, RET_HEADS * RET_DV), row),
            full(wg), full(wn), full(wr), full(wm),
        ],
        out_specs=pl.BlockSpec((1, tm, D), row),
        out_shape=jax.ShapeDtypeStruct((B, S, D), F32),
        compiler_params=_cparams(("arbitrary", "arbitrary")),
        name="mixer_merge",
    )(x, nw, oa, o_r, wg, wn, wr, wm)


def _ffn_kernel(x_ref, halo_ref, nw_ref, wu_ref, cw_ref, cb_ref, wd_ref, fw_ref, o_ref,
                hb_ref, u_a, u_b, acc_ref, *, final_norm):
    tm = x_ref.shape[1]
    tf = wd_ref.shape[1]
    nt = wu_ref.shape[0]
    x = x_ref[0]
    hb_ref[:HALO, :] = jnp.where(pl.program_id(1) > 0, _rms(halo_ref[0], nw_ref[...]), 0.0).astype(BF16)
    hb_ref[HALO:, :] = _rms(x, nw_ref[...]).astype(BF16)
    acc_ref[...] = jnp.zeros_like(acc_ref)
    slots = (u_a, u_b)

    def up(j, slot):
        slots[slot][...] = jnp.dot(hb_ref[...], wu_ref[j], preferred_element_type=F32)

    def down(j, slot):
        u_ref = slots[slot]
        cw = cw_ref[j]
        y = cb_ref[j]
        for k in range(CONV_WIDTH):
            lo = HALO - (CONV_WIDTH - 1) + k
            y = y + cw[k:k + 1, :] * u_ref[lo:lo + tm, :]
        act = (jax.nn.silu(y[:, :tf]) * y[:, tf:]).astype(BF16)
        acc_ref[...] += jnp.dot(act, wd_ref[j], preferred_element_type=F32)

    up(0, 0)

    def pair(p, carry):
        up(2 * p + 1, 1)
        down(2 * p, 0)
        up(2 * p + 2, 0)
        down(2 * p + 1, 1)
        return carry

    lax.fori_loop(0, (nt - 1) // 2, pair, 0)
    if (nt - 1) % 2 == 1:
        up(nt - 1, 1)
        down(nt - 2, 0)
        down(nt - 1, 1)
    else:
        down(nt - 1, 0)
    y = x + acc_ref[...]
    o_ref[0] = _rms(y, fw_ref[...]) if final_norm else y


def _ffn(x, norm_w, w_up, conv_w, conv_b, w_down, final_w, final_norm):
    B, S, D = x.shape
    tm, tf = FFN_TM, FFN_TF
    dff = w_down.shape[0]
    nt = dff // tf
    pair = lambda a: jnp.concatenate([a[..., :dff].reshape(a.shape[:-1] + (nt, tf)),
                                      a[..., dff:].reshape(a.shape[:-1] + (nt, tf))], axis=-1)
    wu = jnp.swapaxes(pair(w_up.astype(BF16)), 0, 1)
    cw = jnp.swapaxes(pair(conv_w), 0, 1)
    cb = pair(conv_b)[:, None, :]
    wd = w_down.astype(BF16).reshape(nt, tf, D)
    row = lambda b, i: (b, i, 0)
    full = lambda a: pl.BlockSpec(a.shape, lambda b, i: (0,) * a.ndim)
    nw = norm_w.reshape(1, D)
    fw = final_w.reshape(1, D)
    return pl.pallas_call(
        functools.partial(_ffn_kernel, final_norm=final_norm),
        grid=(B, S // tm),
        in_specs=[
            pl.BlockSpec((1, tm, D), row),
            pl.BlockSpec((1, HALO, D), lambda b, i: (b, jnp.maximum(i * (tm // HALO) - 1, 0), 0)),
            full(nw), full(wu), full(cw), full(cb), full(wd), full(fw),
        ],
        out_specs=pl.BlockSpec((1, tm, D), row),
        out_shape=jax.ShapeDtypeStruct((B, S, D), F32),
        scratch_shapes=[
            pltpu.VMEM((HALO + tm, D), BF16),
            pltpu.VMEM((HALO + tm, 2 * tf), F32),
            pltpu.VMEM((HALO + tm, 2 * tf), F32),
            pltpu.VMEM((tm, D), F32),
        ],
        compiler_params=_cparams(("arbitrary", "arbitrary")),
        name="conv_ffn",
    )(x, x, nw, wu, cw, cb, wd, fw)


def kernel(x, norm_mix_w, w_in, cmp_pe_k, cmp_w1_k, cmp_w2_k, cmp_pe_v, cmp_w1_v, cmp_w2_v, w_nsa_branch, ret_gn_w, w_ret_branch, w_mix_out, norm_ffn_w, w_ffn_up, ffn_conv_w, ffn_conv_b, w_ffn_down, norm_final_w):
    depth = w_in.shape[0]
    d = x.shape[-1]
    for layer in range(depth):
        qT, kvc, kaug, kwin, vsT, vwT, gT, qr, krT, vr, gr = _projection(x, norm_mix_w[layer], w_in[layer])
        kc, vcT = _compress(kvc, cmp_pe_k[layer], cmp_w1_k[layer], cmp_w2_k[layer],
                            cmp_pe_v[layer], cmp_w1_v[layer], cmp_w2_v[layer])
        oa = _attention(qT, kaug, vsT, kwin, vwT, kc, vcT, gT)
        o_r = _retention(qr, krT, vr, gr, ret_gn_w[layer])
        wg = w_in[layer][:, -2 * d:].astype(BF16)
        x1 = _mix(x, norm_mix_w[layer], oa, o_r, wg, w_nsa_branch[layer].astype(BF16),
                  w_ret_branch[layer].astype(BF16), w_mix_out[layer].astype(BF16))
        x = _ffn(x1, norm_ffn_w[layer], w_ffn_up[layer], ffn_conv_w[layer], ffn_conv_b[layer],
                 w_ffn_down[layer], norm_final_w, final_norm=layer == depth - 1)
    return x
```

```python
import functools

import numpy as np
import jax
import jax.numpy as jnp
from jax import lax
from jax.experimental import pallas as pl
from jax.experimental.pallas import tpu as pltpu

F32 = jnp.float32
BF16 = jnp.bfloat16

NSA_HEADS = 8
NSA_KV = 2
NSA_G = NSA_HEADS // NSA_KV
DH = 64
ROPE_DIM = DH // 4
ROPE_THETA = 500000.0
CMP_BLOCK = 32
CMP_STRIDE = 16
CMP_HIDDEN = 256
SLC_BLOCK = 64
SLC_TOPK = 16
WINDOW = 512
RET_HEADS = 4
RET_DK = 128
RET_DV = 256
RET_THETA = 10000.0
CONV_WIDTH = 3
EPS = 1e-6
NEG = -1e30
LOG2E = 1.4426950408889634

LANES = 128
VMEM_LIMIT = 56 * 1024 * 1024

PROJ_TM = 256
ATT_SQ = 256
ATT_CH = 512
ONES_ROWS = 16
RET_C = 256
MIX_TM = 512
FFN_TM = 512
FFN_TF = 256
HALO = 16


def _cparams(sem):
    return pltpu.CompilerParams(dimension_semantics=sem, vmem_limit_bytes=VMEM_LIMIT)


def _rms(x, w):
    return x * lax.rsqrt(jnp.mean(x * x, axis=-1, keepdims=True) + EPS) * w


_C_Q = (0, 512)
_C_KV = (512, 1280)
_C_G = (1280, 1408)
_C_QR = (1408, 1920)
_C_KR = (1920, 2432)
_C_VR = (2432, 3456)
_C_GR = (3456, 4480)
_PROJ_COLS = 4480


def _proj_kernel(x_ref, nw_ref, w_ref, cn_ref, s1_ref, s2_ref, cr_ref, sr_ref,
                 qT_ref, kvc_ref, kslc_ref, kwin_ref, vsT_ref, vwT_ref, gT_ref,
                 qr_ref, krT_ref, vr_ref, gr_ref, *, seq):
    tm = x_ref.shape[1]
    i = pl.program_id(1)
    hb = _rms(x_ref[0], nw_ref[...]).astype(BF16)

    def mm(c):
        return jnp.dot(hb, w_ref[:, c[0]:c[1]], preferred_element_type=F32)

    cn, s1, s2 = cn_ref[...], s1_ref[...], s2_ref[...]

    def rope_nsa(t):
        return t * cn + pltpu.roll(t, 8, 1) * s1 + pltpu.roll(t, LANES - 8, 1) * s2

    acc = mm(_C_Q)
    for j in range(4):
        t = rope_nsa(acc[:, j * LANES:(j + 1) * LANES]) * (DH ** -0.5 * LOG2E)
        qT_ref[0, 2 * j:2 * j + 2] = t.T.astype(BF16).reshape(2, DH, tm)

    acc = mm(_C_KV)
    part = lambda r: acc[:, r * LANES:(r + 1) * LANES]
    for kind in range(2):
        kvc_ref[0, kind] = part(kind)

    lane = lax.broadcasted_iota(jnp.int32, (tm, LANES), 1)
    pos = i * tm + lax.broadcasted_iota(jnp.int32, (tm, LANES), 0)
    onehot = jnp.where(lane - DH == pos // SLC_BLOCK, 1.0, 0.0)
    for src, dst, fill in ((2, kslc_ref, onehot), (4, kwin_ref, 0.0)):
        t = rope_nsa(part(src))
        for h, th in enumerate((t, pltpu.roll(t, DH, 1))):
            dst[0, h] = jnp.where(lane >= DH, fill, th).astype(BF16)
    vsT_ref[0] = part(3).T.astype(BF16).reshape(NSA_KV, DH, tm)
    vwT_ref[0] = part(5).T.astype(BF16).reshape(NSA_KV, DH, tm)

    g = jax.nn.sigmoid(mm(_C_G)).T
    for h in range(NSA_KV):
        rows = [g[(h * NSA_G + m) * 3 + r:(h * NSA_G + m) * 3 + r + 1] for r in range(3) for m in range(NSA_G)]
        gT_ref[0, h] = jnp.concatenate(rows + rows[:4], axis=0)

    cr, sr = cr_ref[...], sr_ref[...]

    def rope_ret(t):
        return t * cr + pltpu.roll(t, RET_DK // 2, 1) * sr

    acc = mm(_C_QR)
    for j in range(RET_HEADS):
        qr_ref[0, :, j * LANES:(j + 1) * LANES] = rope_ret(acc[:, j * LANES:(j + 1) * LANES]).astype(BF16)
    acc = mm(_C_KR)
    for j in range(RET_HEADS):
        t = rope_ret(acc[:, j * LANES:(j + 1) * LANES]) * (RET_DK ** -0.5)
        krT_ref[0, j] = t.T.astype(BF16)
    vr_ref[0] = mm(_C_VR).astype(BF16)
    gr_ref[0] = mm(_C_GR).astype(BF16)


def _pack_proj_weight(w_in):
    d = w_in.shape[0]
    g1 = NSA_HEADS * DH + 6 * NSA_KV * DH + 3 * NSA_HEADS
    r1 = g1 + 2 * RET_HEADS * RET_DK + 2 * RET_HEADS * RET_DV
    out = jnp.concatenate([w_in[:, :g1].astype(BF16), jnp.zeros((d, _C_G[1] - g1), BF16),
                           w_in[:, g1:r1].astype(BF16)], axis=1)
    assert out.shape[1] == _PROJ_COLS
    return out


def _rope_tables(seq):
    pos = jnp.arange(seq, dtype=F32)[:, None]
    half = ROPE_DIM // 2
    inv = jnp.power(jnp.float32(ROPE_THETA), -jnp.arange(half, dtype=F32) / half)
    inv_head = jnp.concatenate([inv, inv, jnp.zeros((DH - ROPE_DIM,), F32)])
    ang = pos * jnp.concatenate([inv_head, inv_head])[None, :]
    d = np.arange(LANES) % DH
    cn = jnp.cos(ang)
    s1 = jnp.sin(ang) * jnp.asarray((d >= half) & (d < ROPE_DIM), F32)
    s2 = jnp.sin(ang) * jnp.asarray(-1.0 * (d < half), F32)
    half_r = RET_DK // 2
    inv_r = jnp.power(jnp.float32(RET_THETA), -jnp.arange(half_r, dtype=F32) / half_r)
    ang_r = pos * jnp.concatenate([inv_r, inv_r])[None, :]
    cr = jnp.cos(ang_r)
    sr = jnp.sin(ang_r) * jnp.asarray(np.where(np.arange(RET_DK) < half_r, -1.0, 1.0), F32)
    return cn, s1, s2, cr, sr


def _projection(x, norm_w, w_in):
    B, S, D = x.shape
    tm = PROJ_TM
    wp = _pack_proj_weight(w_in)
    tabs = _rope_tables(S)
    row = lambda b, i: (b, i, 0)
    tab_spec = pl.BlockSpec((tm, LANES), lambda b, i: (i, 0))
    out_shape = (
        jax.ShapeDtypeStruct((B, NSA_HEADS, DH, S), BF16),
        jax.ShapeDtypeStruct((B, 2, S, NSA_KV * DH), F32),
        jax.ShapeDtypeStruct((B, NSA_KV, S, LANES), BF16),
        jax.ShapeDtypeStruct((B, NSA_KV, S, LANES), BF16),
        jax.ShapeDtypeStruct((B, NSA_KV, DH, S), BF16),
        jax.ShapeDtypeStruct((B, NSA_KV, DH, S), BF16),
        jax.ShapeDtypeStruct((B, NSA_KV, 16, S), F32),
        jax.ShapeDtypeStruct((B, S, RET_HEADS * RET_DK), BF16),
        jax.ShapeDtypeStruct((B, RET_HEADS, RET_DK, S), BF16),
        jax.ShapeDtypeStruct((B, S, RET_HEADS * RET_DV), BF16),
        jax.ShapeDtypeStruct((B, S, RET_HEADS * RET_DV), BF16),
    )
    tok_last = lambda n, d: pl.BlockSpec((1, n, d, tm), lambda b, i: (b, 0, 0, i))
    out_specs = (
        tok_last(NSA_HEADS, DH),
        pl.BlockSpec((1, 2, tm, NSA_KV * DH), lambda b, i: (b, 0, i, 0)),
        pl.BlockSpec((1, NSA_KV, tm, LANES), lambda b, i: (b, 0, i, 0)),
        pl.BlockSpec((1, NSA_KV, tm, LANES), lambda b, i: (b, 0, i, 0)),
        tok_last(NSA_KV, DH),
        tok_last(NSA_KV, DH),
        tok_last(NSA_KV, 16),
        pl.BlockSpec((1, tm, RET_HEADS * RET_DK), row),
        tok_last(RET_HEADS, RET_DK),
        pl.BlockSpec((1, tm, RET_HEADS * RET_DV), row),
        pl.BlockSpec((1, tm, RET_HEADS * RET_DV), row),
    )
    return pl.pallas_call(
        functools.partial(_proj_kernel, seq=S),
        grid=(B, S // tm),
        in_specs=[
            pl.BlockSpec((1, tm, D), row),
            pl.BlockSpec((1, D), lambda b, i: (0, 0)),
            pl.BlockSpec((D, _PROJ_COLS), lambda b, i: (0, 0)),
            tab_spec, tab_spec, tab_spec, tab_spec, tab_spec,
        ],
        out_specs=out_specs,
        out_shape=out_shape,
        compiler_params=_cparams(("arbitrary", "arbitrary")),
        name="in_proj",
    )(x, norm_w.reshape(1, D), wp, *tabs)


def _cmp_kernel(kvc_ref, pek_ref, pev_ref, w1k_ref, w1v_ref, w2k_ref, w2v_ref, kc_ref, vcT_ref):
    nseg = kvc_ref.shape[2] // CMP_STRIDE
    half = CMP_BLOCK // 2
    row = lax.broadcasted_iota(jnp.int32, (nseg, 1), 0)
    for kind, (pe_ref, w1_ref, w2_ref) in enumerate(((pek_ref, w1k_ref, w2k_ref), (pev_ref, w1v_ref, w2v_ref))):
        acc_a = [jnp.zeros((nseg, CMP_HIDDEN), F32) for _ in range(NSA_KV)]
        acc_b = [jnp.zeros((nseg, CMP_HIDDEN), F32) for _ in range(NSA_KV)]
        for l in range(half):
            xl = kvc_ref[0, kind, pl.ds(l, nseg, stride=CMP_STRIDE), :]
            xa = (xl + pe_ref[l:l + 1, :]).astype(BF16)
            xb = (xl + pe_ref[half + l:half + l + 1, :]).astype(BF16)
            for h in range(NSA_KV):
                acc_a[h] += jnp.dot(xa[:, h * DH:(h + 1) * DH], w1_ref[l], preferred_element_type=F32)
                acc_b[h] += jnp.dot(xb[:, h * DH:(h + 1) * DH], w1_ref[half + l], preferred_element_type=F32)
        for h in range(NSA_KV):
            nxt = jnp.concatenate([acc_b[h][1:], jnp.zeros((1, CMP_HIDDEN), F32)], axis=0)
            hid = jax.nn.gelu(acc_a[h] + nxt).astype(BF16)
            out = jnp.dot(hid, w2_ref[...], preferred_element_type=F32)
            out = jnp.where(row < nseg - 1, out, 0.0)
            if kind == 0:
                kc_ref[0, h] = jnp.concatenate([out, jnp.zeros((nseg, LANES - DH), F32)], axis=1).astype(BF16)
            else:
                vcT_ref[0, h] = out.T.astype(BF16)


def _compress(kvc, pe_k, w1_k, w2_k, pe_v, w1_v, w2_v):
    B, _, S, _ = kvc.shape
    nseg = S // CMP_STRIDE
    tile2 = lambda t: jnp.concatenate([t, t], axis=1)
    w1 = lambda w: w.astype(BF16).reshape(CMP_BLOCK, DH, CMP_HIDDEN)
    full = lambda *shape: pl.BlockSpec(shape, lambda b: (0,) * len(shape))
    return pl.pallas_call(
        _cmp_kernel,
        grid=(B,),
        in_specs=[
            pl.BlockSpec((1, 2, S, NSA_KV * DH), lambda b: (b, 0, 0, 0)),
            full(CMP_BLOCK, LANES), full(CMP_BLOCK, LANES),
            full(CMP_BLOCK, DH, CMP_HIDDEN), full(CMP_BLOCK, DH, CMP_HIDDEN),
            full(CMP_HIDDEN, DH), full(CMP_HIDDEN, DH),
        ],
        out_specs=(
            pl.BlockSpec((1, NSA_KV, nseg, LANES), lambda b: (b, 0, 0, 0)),
            pl.BlockSpec((1, NSA_KV, DH, nseg), lambda b: (b, 0, 0, 0)),
        ),
        out_shape=(
            jax.ShapeDtypeStruct((B, NSA_KV, nseg, LANES), BF16),
            jax.ShapeDtypeStruct((B, NSA_KV, DH, nseg), BF16),
        ),
        compiler_params=_cparams(("arbitrary",)),
        name="nsa_compress",
    )(kvc, tile2(pe_k), tile2(pe_v), w1(w1_k), w1(w1_v), w2_k.astype(BF16), w2_v.astype(BF16))


def _overlap_matrix(seq):
    nseg = seq // CMP_STRIDE
    n_slc = seq // SLC_BLOCK
    cs = np.arange(nseg) * CMP_STRIDE
    ss = np.arange(n_slc) * SLC_BLOCK
    ov = np.clip(np.minimum(cs[None, :] + CMP_BLOCK, ss[:, None] + SLC_BLOCK)
                 - np.maximum(cs[None, :], ss[:, None]), 0, None) / CMP_BLOCK
    ov[:, nseg - 1] = 0.0
    return ov


def _with_ones(vT):
    return jnp.concatenate([vT, jnp.ones((ONES_ROWS, vT.shape[1]), BF16)], axis=0)


def _att_kernel(qT_ref, ka_ref, vsT_ref, kw_ref, vwT_ref, kc_ref, vcT_ref, g_ref, ov_ref, wb_ref, cb_ref, o_ref,
                s_a, s_b, cm_a, cm_b, m_scr, acc_scr, sw_scr, imp_scr, rank_scr):
    sq = qT_ref.shape[3]
    nq4 = NSA_G * sq
    n_slc = ov_ref.shape[0]
    nseg = kc_ref.shape[2]
    i = pl.program_id(2)
    q0 = pl.multiple_of(i * sq, sq)

    qT4 = jnp.concatenate([qT_ref[0, g] for g in range(NSA_G)], axis=1)
    qw = jnp.concatenate([qT4, jnp.zeros((LANES - DH, nq4), BF16)], axis=0)
    tq = q0 + lax.broadcasted_iota(jnp.int32, (1, sq), 1)

    def band_bias(key0, n_tiles, width):
        dq = (q0 - key0) // sq
        tiles = []
        for a in range(n_tiles):
            delta = dq - a
            kind = jnp.where(delta == 0, 1, jnp.where(delta == width, 3, 2))
            kind = jnp.where(jnp.logical_or(delta < 0, delta > width), 0, kind)
            tiles.append(jnp.concatenate([wb_ref[kind]] * NSA_G, axis=1))
        return jnp.concatenate(tiles, axis=0)

    nk = WINDOW + sq
    ks = pl.multiple_of(jnp.maximum(q0 - WINDOW, 0), sq)
    s = jnp.dot(kw_ref[0, 0, pl.ds(ks, nk), :], qw, preferred_element_type=F32)
    s = s + band_bias(ks, nk // sq, WINDOW // sq)
    sw_scr[...] = s
    m_win = jnp.max(s, axis=0, keepdims=True)

    s = jnp.dot(kc_ref[0, 0], qw, preferred_element_type=F32)
    cb = cb_ref[pl.ds(pl.multiple_of(nseg - q0 // CMP_STRIDE, CMP_STRIDE), nseg), :]
    s = s + jnp.concatenate([cb] * NSA_G, axis=1)
    m = jnp.maximum(jnp.max(s, axis=0, keepdims=True), 0.1 * NEG)
    p = jnp.exp2(s - m)
    l = jnp.sum(p, axis=0, keepdims=True)
    pn = p * jnp.where(l > 0.0, 1.0 / l, 0.0)
    o_cmp = jnp.dot(vcT_ref[0, 0], pn.astype(BF16), preferred_element_type=F32)

    psum = pn[:, 0:sq]
    for g in range(1, NSA_G):
        psum = psum + pn[:, g * sq:(g + 1) * sq]
    p_hi = psum.astype(BF16)
    p_lo = (psum - p_hi.astype(F32)).astype(BF16)
    imp = (jnp.dot(ov_ref[...], p_hi, preferred_element_type=F32)
           + jnp.dot(ov_ref[...], p_lo, preferred_element_type=F32))
    jb = lax.broadcasted_iota(jnp.int32, (n_slc, 1), 0)
    cur = tq // SLC_BLOCK
    forced = jnp.logical_or(jb == 0, jnp.logical_or(jb == cur, jb == cur - 1))
    imp = jnp.where(forced, jnp.inf, jnp.where(jb <= cur, imp, -jnp.inf))
    imp_scr[...] = imp
    rank_scr[...] = jnp.zeros_like(rank_scr)
    sub = lax.broadcasted_iota(jnp.int32, (8, 1), 0)
    g_last = ((q0 + sq - 1) // SLC_BLOCK) // 8

    def count(vp, v):
        x = imp_scr[8 * v:8 * v + 8, :]
        acc = rank_scr[8 * v:8 * v + 8, :]
        for jp in range(8 * vp, 8 * vp + 8):
            r = imp_scr[jp:jp + 1, :]
            if vp < v:
                cnt = jnp.where(r >= x, 1.0, 0.0)
            elif vp > v:
                cnt = jnp.where(r > x, 1.0, 0.0)
            else:
                cnt = jnp.where(sub > jp - 8 * v, jnp.where(r >= x, 1.0, 0.0), jnp.where(r > x, 1.0, 0.0))
            acc = acc + cnt
        rank_scr[8 * v:8 * v + 8, :] = acc

    for g in range(n_slc // 8):
        @pl.when(g <= g_last)
        def _():
            for v in range(g + 1):
                count(g, v)
            for vp in range(g):
                count(vp, g)

    bias = jnp.where(rank_scr[...] < float(min(SLC_TOPK, n_slc)), 0.0, NEG).astype(BF16)
    if n_slc < LANES - DH:
        bias = jnp.concatenate([bias, jnp.zeros((LANES - DH - n_slc, sq), BF16)], axis=0)
    qa = jnp.concatenate([qT4, jnp.concatenate([bias] * NSA_G, axis=1)], axis=0)

    ch = s_a.shape[0]
    last = (i * sq) // ch
    slots = ((s_a, cm_a), (s_b, cm_b))

    def produce(c, slot, causal=False):
        s_ref, cm_ref = slots[slot]
        k0 = pl.multiple_of(c * ch, ch)
        s = jnp.dot(ka_ref[0, 0, pl.ds(k0, ch), :], qa, preferred_element_type=F32)
        if causal:
            s = s + band_bias(k0, ch // sq, nseg * CMP_STRIDE // sq)
        s_ref[...] = s
        cm_ref[...] = jnp.max(s, axis=0, keepdims=True)

    def consume(c, slot):
        s_ref, cm_ref = slots[slot]
        k0 = pl.multiple_of(c * ch, ch)
        m_old = m_scr[...]
        m_new = jnp.maximum(m_old, cm_ref[...])
        p = jnp.exp2(s_ref[...] - m_new).astype(BF16)
        pv = jnp.dot(_with_ones(vsT_ref[0, 0, :, pl.ds(k0, ch)]), p, preferred_element_type=F32)
        acc_scr[...] = jnp.exp2(m_old - m_new) * acc_scr[...] + pv
        m_scr[...] = m_new

    m_scr[...] = jnp.full_like(m_scr, -jnp.inf)
    acc_scr[...] = jnp.zeros_like(acc_scr)
    produce(last, 0, causal=True)

    p = jnp.exp2(sw_scr[...] - m_win).astype(BF16)
    r = jnp.dot(_with_ones(vwT_ref[0, 0, :, pl.ds(ks, nk)]), p, preferred_element_type=F32)
    o_win = r[:DH] * (1.0 / r[DH:DH + 1])

    def pair(j, carry):
        produce(2 * j, 1)
        consume(jnp.where(j == 0, last, 2 * j - 1), 0)
        produce(2 * j + 1, 0)
        consume(2 * j, 1)
        return carry

    npairs = last // 2
    lax.fori_loop(0, npairs, pair, 0)
    held = jnp.where(npairs == 0, last, 2 * npairs - 1)

    @pl.when(last % 2 == 1)
    def _():
        produce(last - 1, 1)
        consume(held, 0)
        consume(last - 1, 1)

    @pl.when(last % 2 == 0)
    def _():
        consume(held, 0)

    o_slc = acc_scr[:DH] * (1.0 / acc_scr[DH:DH + 1])

    def gate(r):
        return jnp.concatenate([g_ref[0, 0, r * NSA_G + g:r * NSA_G + g + 1, :] for g in range(NSA_G)], axis=1)

    o = gate(0) * o_cmp + gate(1) * o_slc + gate(2) * o_win
    o = jnp.concatenate([o[:, g * sq:(g + 1) * sq] for g in range(NSA_G)], axis=0)
    o_ref[0, 0] = o.T.astype(BF16)


def _attention(qT, kaug, vsT, kwin, vwT, kc, vcT, gT):
    B, _, _, S = qT.shape
    sq = ATT_SQ
    nseg = S // CMP_STRIDE
    n_slc = S // SLC_BLOCK
    ov = jnp.asarray(_overlap_matrix(S), BF16)
    rr = np.arange(sq)[:, None]
    tl = np.arange(sq)[None, :]
    vis = np.stack([np.zeros((sq, sq), bool), rr <= tl, np.ones((sq, sq), bool), rr > tl])
    wb = jnp.asarray(np.where(vis, 0.0, NEG), F32)
    cj = np.arange(2 * nseg)[:, None] - nseg
    cb = jnp.asarray(np.where(CMP_STRIDE * cj + (CMP_BLOCK - 1) <= tl, 0.0, NEG), F32)
    per_bh = lambda *shape: pl.BlockSpec((1, 1) + shape, lambda b, h, i: (b, h, 0, 0))
    return pl.pallas_call(
        _att_kernel,
        grid=(B, NSA_KV, S // sq),
        in_specs=[
            pl.BlockSpec((1, NSA_G, DH, sq), lambda b, h, i: (b, h, 0, i)),
            per_bh(S, LANES), per_bh(DH, S), per_bh(S, LANES), per_bh(DH, S),
            per_bh(nseg, LANES), per_bh(DH, nseg),
            pl.BlockSpec((1, 1, 16, sq), lambda b, h, i: (b, h, 0, i)),
            pl.BlockSpec((n_slc, nseg), lambda b, h, i: (0, 0)),
            pl.BlockSpec((4, sq, sq), lambda b, h, i: (0, 0, 0)),
            pl.BlockSpec((2 * nseg, sq), lambda b, h, i: (0, 0)),
        ],
        out_specs=pl.BlockSpec((1, 1, sq, NSA_G * DH), lambda b, h, i: (b, h, i, 0)),
        out_shape=jax.ShapeDtypeStruct((B, NSA_KV, S, NSA_G * DH), BF16),
        scratch_shapes=[
            pltpu.VMEM((ATT_CH, NSA_G * sq), F32),
            pltpu.VMEM((ATT_CH, NSA_G * sq), F32),
            pltpu.VMEM((1, NSA_G * sq), F32),
            pltpu.VMEM((1, NSA_G * sq), F32),
            pltpu.VMEM((1, NSA_G * sq), F32),
            pltpu.VMEM((DH + ONES_ROWS, NSA_G * sq), F32),
            pltpu.VMEM((WINDOW + sq, NSA_G * sq), F32),
            pltpu.VMEM((n_slc, sq), F32),
            pltpu.VMEM((n_slc, sq), F32),
        ],
        compiler_params=_cparams(("arbitrary", "arbitrary", "arbitrary")),
        name="nsa_attention",
    )(qT, kaug, vsT, kwin, vwT, kc, vcT, gT, ov, wb, cb)


def _ret_kernel(q_ref, kT_ref, v_ref, g_ref, din_ref, qd_ref, kd_ref, cd_ref, gn_ref, o_ref, state_ref):
    @pl.when(pl.program_id(1) == 0)
    def _():
        state_ref[...] = jnp.zeros_like(state_ref)

    for h in range(RET_HEADS):
        q = q_ref[0, :, h * RET_DK:(h + 1) * RET_DK]
        kT = kT_ref[0, h]
        v = v_ref[0, :, h * RET_DV:(h + 1) * RET_DV]
        state = state_ref[h]
        sc = jnp.dot(q, kT, preferred_element_type=F32) * din_ref[h]
        o = jnp.dot(sc.astype(BF16), v, preferred_element_type=F32)
        o = o + jnp.dot(q, state.astype(BF16), preferred_element_type=F32) * qd_ref[h]
        kd = (kT.astype(F32) * kd_ref[h]).astype(BF16)
        state_ref[h] = state * cd_ref[h] + jnp.dot(kd, v, preferred_element_type=F32)

        mean = jnp.mean(o, axis=-1, keepdims=True)
        var = jnp.mean(jnp.square(o - mean), axis=-1, keepdims=True)
        y = (o - mean) * lax.rsqrt(var + EPS) * gn_ref[h]
        g = g_ref[0, :, h * RET_DV:(h + 1) * RET_DV].astype(F32)
        o_ref[0, :, h * RET_DV:(h + 1) * RET_DV] = (y * jax.nn.silu(g)).astype(BF16)


def _retention(qr, krT, vr, gr, gn_w):
    B, S, _ = qr.shape
    C = RET_C
    H = RET_HEADS
    log_gamma = jnp.log1p(-jnp.exp2(-5.0 - jnp.arange(H, dtype=F32)))
    idx = jnp.arange(C, dtype=F32)
    diff = idx[:, None] - idx[None, :]
    din = jnp.where(diff >= 0, jnp.exp(log_gamma[:, None, None] * jnp.maximum(diff, 0.0)), 0.0)
    qd = jnp.exp(log_gamma[:, None] * (idx[None, :] + 1.0))
    qd = jnp.broadcast_to(qd[:, :, None], (H, C, RET_DV))
    kd = jnp.exp(log_gamma[:, None] * (C - 1.0 - idx[None, :]))[:, None, :]
    cd = jnp.broadcast_to(jnp.exp(log_gamma * C)[:, None, None], (H, 1, RET_DV))
    all_h = lambda *shape: pl.BlockSpec((H,) + shape, lambda b, n: (0, 0, 0))
    rows = lambda w: pl.BlockSpec((1, C, w), lambda b, n: (b, n, 0))
    return pl.pallas_call(
        _ret_kernel,
        grid=(B, S // C),
        in_specs=[
            rows(H * RET_DK),
            pl.BlockSpec((1, H, RET_DK, C), lambda b, n: (b, 0, 0, n)),
            rows(H * RET_DV),
            rows(H * RET_DV),
            all_h(C, C), all_h(C, RET_DV), all_h(1, C), all_h(1, RET_DV), all_h(1, RET_DV),
        ],
        out_specs=rows(H * RET_DV),
        out_shape=jax.ShapeDtypeStruct((B, S, H * RET_DV), BF16),
        scratch_shapes=[pltpu.VMEM((H, RET_DK, RET_DV), F32)],
        compiler_params=_cparams(("arbitrary", "arbitrary")),
        name="retention",
    )(qr, krT, vr, gr, din, qd, kd, cd, gn_w.reshape(H, 1, RET_DV))


def _mix_kernel(x_ref, nw_ref, oa_ref, or_ref, wg_ref, wn_ref, wr_ref, wm_ref, o_ref):
    d = x_ref.shape[2]
    x = x_ref[0]
    hb = _rms(x, nw_ref[...]).astype(BF16)
    gate = jax.nn.sigmoid(jnp.dot(hb, wg_ref[...], preferred_element_type=F32))
    kw = NSA_G * DH
    y_a = jnp.dot(oa_ref[0, 0], wn_ref[0:kw, :], preferred_element_type=F32)
    for h in range(1, NSA_KV):
        y_a = y_a + jnp.dot(oa_ref[0, h], wn_ref[h * kw:(h + 1) * kw, :], preferred_element_type=F32)
    y_b = jnp.dot(or_ref[0], wr_ref[...], preferred_element_type=F32)
    mix = (gate[:, :d] * y_a + gate[:, d:] * y_b).astype(BF16)
    o_ref[0] = x + jnp.dot(mix, wm_ref[...], preferred_element_type=F32)


def _mix(x, norm_w, oa, o_r, wg, wn, wr, wm):
    B, S, D = x.shape
    tm = MIX_TM
    row = lambda b, i: (b, i, 0)
    full = lambda a: pl.BlockSpec(a.shape, lambda b, i: (0,) * a.ndim)
    nw = norm_w.reshape(1, D)
    return pl.pallas_call(
        _mix_kernel,
        grid=(B, S // tm),
        in_specs=[
            pl.BlockSpec((1, tm, D), row), full(nw),
            pl.BlockSpec((1, NSA_KV, tm, NSA_G * DH), lambda b, i: (b, 0, i, 0)),
            pl.BlockSpec((1, tm, RET_HEADS * RET_DV), row),
            full(wg), full(wn), full(wr), full(wm),
        ],
        out_specs=pl.BlockSpec((1, tm, D), row),
        out_shape=jax.ShapeDtypeStruct((B, S, D), F32),
        compiler_params=_cparams(("arbitrary", "arbitrary")),
        name="mixer_merge",
    )(x, nw, oa, o_r, wg, wn, wr, wm)


def _ffn_kernel(x_ref, halo_ref, nw_ref, wu_ref, cw_ref, cb_ref, wd_ref, fw_ref, o_ref,
                hb_ref, u_a, u_b, acc_ref, *, final_norm):
    tm = x_ref.shape[1]
    tf = wd_ref.shape[1]
    nt = wd_ref.shape[0]
    x = x_ref[0]
    hb_ref[:HALO, :] = jnp.where(pl.program_id(1) > 0, _rms(halo_ref[0], nw_ref[...]), 0.0).astype(BF16)
    hb_ref[HALO:, :] = _rms(x, nw_ref[...]).astype(BF16)
    acc_ref[...] = jnp.zeros_like(acc_ref)
    slots = (u_a, u_b)

    dff = nt * tf

    def cols(j, half):
        return pl.ds(pl.multiple_of(half * dff + j * tf, LANES), tf)

    def up(j, slot):
        for half in range(2):
            slots[slot][:, half * tf:(half + 1) * tf] = jnp.dot(
                hb_ref[...], wu_ref[:, cols(j, half)], preferred_element_type=F32)

    def down(j, slot):
        u_ref = slots[slot]
        ys = []
        for half in range(2):
            cw = cw_ref[:, cols(j, half)]
            y = cb_ref[:, cols(j, half)]
            for k in range(CONV_WIDTH):
                lo = HALO - (CONV_WIDTH - 1) + k
                y = y + cw[k:k + 1, :] * u_ref[lo:lo + tm, half * tf:(half + 1) * tf]
            ys.append(y)
        act = (jax.nn.silu(ys[0]) * ys[1]).astype(BF16)
        acc_ref[...] += jnp.dot(act, wd_ref[j], preferred_element_type=F32)

    up(0, 0)

    def pair(p, carry):
        up(2 * p + 1, 1)
        down(2 * p, 0)
        up(2 * p + 2, 0)
        down(2 * p + 1, 1)
        return carry

    lax.fori_loop(0, (nt - 1) // 2, pair, 0)
    if (nt - 1) % 2 == 1:
        up(nt - 1, 1)
        down(nt - 2, 0)
        down(nt - 1, 1)
    else:
        down(nt - 1, 0)
    y = x + acc_ref[...]
    o_ref[0] = _rms(y, fw_ref[...]) if final_norm else y


def _ffn(x, norm_w, w_up, conv_w, conv_b, w_down, final_w, final_norm):
    B, S, D = x.shape
    tm, tf = FFN_TM, FFN_TF
    dff = w_down.shape[0]
    nt = dff // tf
    wu = w_up.astype(BF16)
    cw = conv_w
    cb = conv_b.reshape(1, 2 * dff)
    wd = w_down.astype(BF16).reshape(nt, tf, D)
    row = lambda b, i: (b, i, 0)
    full = lambda a: pl.BlockSpec(a.shape, lambda b, i: (0,) * a.ndim)
    nw = norm_w.reshape(1, D)
    fw = final_w.reshape(1, D)
    return pl.pallas_call(
        functools.partial(_ffn_kernel, final_norm=final_norm),
        grid=(B, S // tm),
        in_specs=[
            pl.BlockSpec((1, tm, D), row),
            pl.BlockSpec((1, HALO, D), lambda b, i: (b, jnp.maximum(i * (tm // HALO) - 1, 0), 0)),
            full(nw), full(wu), full(cw), full(cb), full(wd), full(fw),
        ],
        out_specs=pl.BlockSpec((1, tm, D), row),
        out_shape=jax.ShapeDtypeStruct((B, S, D), F32),
        scratch_shapes=[
            pltpu.VMEM((HALO + tm, D), BF16),
            pltpu.VMEM((HALO + tm, 2 * tf), F32),
            pltpu.VMEM((HALO + tm, 2 * tf), F32),
            pltpu.VMEM((tm, D), F32),
        ],
        compiler_params=_cparams(("arbitrary", "arbitrary")),
        name="conv_ffn",
    )(x, x, nw, wu, cw, cb, wd, fw)


def kernel(x, norm_mix_w, w_in, cmp_pe_k, cmp_w1_k, cmp_w2_k, cmp_pe_v, cmp_w1_v, cmp_w2_v, w_nsa_branch, ret_gn_w, w_ret_branch, w_mix_out, norm_ffn_w, w_ffn_up, ffn_conv_w, ffn_conv_b, w_ffn_down, norm_final_w):
    depth = w_in.shape[0]
    d = x.shape[-1]
    for layer in range(depth):
        qT, kvc, kaug, kwin, vsT, vwT, gT, qr, krT, vr, gr = _projection(x, norm_mix_w[layer], w_in[layer])
        kc, vcT = _compress(kvc, cmp_pe_k[layer], cmp_w1_k[layer], cmp_w2_k[layer],
                            cmp_pe_v[layer], cmp_w1_v[layer], cmp_w2_v[layer])
        oa = _attention(qT, kaug, vsT, kwin, vwT, kc, vcT, gT)
        o_r = _retention(qr, krT, vr, gr, ret_gn_w[layer])
        wg = w_in[layer][:, -2 * d:].astype(BF16)
        x1 = _mix(x, norm_mix_w[layer], oa, o_r, wg, w_nsa_branch[layer].astype(BF16),
                  w_ret_branch[layer].astype(BF16), w_mix_out[layer].astype(BF16))
        x = _ffn(x1, norm_ffn_w[layer], w_ffn_up[layer], ffn_conv_w[layer], ffn_conv_b[layer],
                 w_ffn_down[layer], norm_final_w, final_norm=layer == depth - 1)
    return x
```
